```python
import jax, jax.numpy as jnp
from jax import lax
import numpy as np

D_MODEL = 1024
BATCH = 8
SEQ = 4096
DEPTH = 2

CHUNK = 64
MIX_W = D_MODEL
GLA_HEADS = 4
GLA_DV = MIX_W // 2 // GLA_HEADS
GLA_DK = GLA_DV // 2
GLA_QK_W = GLA_HEADS * GLA_DK
GLA_V_W = GLA_HEADS * GLA_DV
GLA_RANK = 16
GLA_GATE_NORM = 16.0
ATT_HEADS = 8
ATT_DH = (MIX_W - GLA_V_W) // ATT_HEADS
ATT_W = ATT_HEADS * ATT_DH
BAND_CHUNKS = 8
BAND = (BAND_CHUNKS + 1) * CHUNK
REL_CLIP = 128
PEER_HEADS = 8
N_KEYS = 128
N_EXPERTS = N_KEYS * N_KEYS
PEER_DK = 256
PEER_TOPK = 16
PEER_BLOCK = 128
PLE_DIM = 256
RMS_EPS = 1e-6
SPLIT_SIZES = (GLA_QK_W, GLA_QK_W, GLA_V_W, GLA_RANK, GLA_V_W, ATT_W, ATT_W, ATT_W)
SPLIT_POINTS = tuple(int(v) for v in np.cumsum(SPLIT_SIZES)[:-1])
IN_W = sum(SPLIT_SIZES)

kernel_name = 'hybrid_gla_bandattn_peer_ple'


def rms_norm(x, g):
    xf = x.astype(jnp.float32)
    y = xf * lax.rsqrt(jnp.mean(xf * xf, axis=-1, keepdims=True) + RMS_EPS)
    return (y * g.astype(jnp.float32)).astype(x.dtype)


def gla_group(q, k, v, g_lr, g_w2, g_b, head_norm, out_gate):
    b, s, _ = q.shape
    nc = s // CHUNK
    f32 = jnp.float32

    def heads(t, d):
        return t.astype(f32).reshape(b, nc, CHUNK, GLA_HEADS, d).transpose(0, 3, 1, 2, 4)

    q = heads(q, GLA_DK) * (GLA_DK ** -0.5)
    k = heads(k, GLA_DK)
    v = heads(v, GLA_DV)
    g = jax.nn.log_sigmoid((g_lr @ g_w2 + g_b).astype(f32)) / GLA_GATE_NORM
    g = heads(g, GLA_DK)
    cum = jnp.cumsum(g, axis=3)
    last = cum[:, :, :, -1:, :]
    mid = cum[:, :, :, CHUNK // 2:CHUNK // 2 + 1, :]
    a = jnp.einsum('bhnid,bhnjd->bhnij', q * jnp.exp(cum - mid), k * jnp.exp(mid - cum))
    causal = jnp.tril(jnp.ones((CHUNK, CHUNK), dtype=bool))
    o_intra = jnp.einsum('bhnij,bhnje->bhnie', jnp.where(causal, a, 0.0), v)
    k_state = k * jnp.exp(last - cum)
    decay = jnp.exp(last[:, :, :, 0, :])

    def step(state, inp):
        kc, vc, dc = inp
        new = dc[..., None] * state + jnp.einsum('bhcd,bhce->bhde', kc, vc)
        return new, state

    xs = (k_state.transpose(2, 0, 1, 3, 4), v.transpose(2, 0, 1, 3, 4), decay.transpose(2, 0, 1, 3))
    init = jnp.zeros((b, GLA_HEADS, GLA_DK, GLA_DV), f32)
    _, prev = lax.scan(step, init, xs)
    o_inter = jnp.einsum('bhnid,nbhde->bhnie', q * jnp.exp(cum), prev)
    o = o_intra + o_inter
    o = o * lax.rsqrt(jnp.mean(o * o, axis=-1, keepdims=True) + RMS_EPS) * head_norm.astype(f32)
    o = o.transpose(0, 2, 3, 1, 4).reshape(b, s, GLA_V_W)
    return (o * jax.nn.silu(out_gate.astype(f32))).astype(out_gate.dtype)


def band_attention(q, k, v, rel_table):
    b, s, _ = q.shape
    nc = s // CHUNK
    pad = BAND_CHUNKS * CHUNK
    f32 = jnp.float32

    def heads(t):
        return t.reshape(b, s, ATT_HEADS, ATT_DH).transpose(0, 2, 1, 3)

    q, k, v = heads(q), heads(k), heads(v)
    kp = jnp.pad(k, ((0, 0), (0, 0), (pad, 0), (0, 0)))
    vp = jnp.pad(v, ((0, 0), (0, 0), (pad, 0), (0, 0)))
    rel = jnp.arange(CHUNK)[:, None] + pad - jnp.arange(BAND)[None, :]
    bias = rel_table.astype(f32)[:, jnp.clip(rel, -REL_CLIP, REL_CLIP) + REL_CLIP]
    scale = ATT_DH ** -0.5
    key_off = jnp.arange(BAND)

    def one_chunk(c):
        start = c * CHUNK
        qc = lax.dynamic_slice_in_dim(q, start, CHUNK, axis=2)
        kc = lax.dynamic_slice_in_dim(kp, start, BAND, axis=2)
        vc = lax.dynamic_slice_in_dim(vp, start, BAND, axis=2)
        sc = jnp.einsum('bhqd,bhkd->bhqk', qc, kc).astype(f32) * scale + bias
        valid = (start - pad + key_off) >= 0
        sc = jnp.where(valid, sc, -1e30)
        pr = jax.nn.softmax(sc, axis=-1).astype(vc.dtype)
        return jnp.einsum('bhqk,bhkd->bhqd', pr, vc)

    out = lax.map(one_chunk, jnp.arange(nc))
    return out.transpose(1, 0, 3, 2, 4).reshape(b, s, ATT_W)


def peer_ffn(xn, w_q, subkeys, u_tab, v_tab):
    b, s, d = xn.shape
    xt = xn.reshape(b * s // PEER_BLOCK, PEER_BLOCK, d)

    def block(xb):
        qh = (xb @ w_q).reshape(PEER_BLOCK, PEER_HEADS, 2, PEER_DK // 2)
        sc = jnp.einsum('thpd,hpnd->thpn', qh, subkeys).astype(jnp.float32)
        s1, i1 = lax.top_k(sc[:, :, 0], PEER_TOPK)
        s2, i2 = lax.top_k(sc[:, :, 1], PEER_TOPK)
        comb = (s1[..., :, None] + s2[..., None, :]).reshape(PEER_BLOCK, PEER_HEADS, PEER_TOPK * PEER_TOPK)
        top, pos = lax.top_k(comb, PEER_TOPK)
        e = (jnp.take_along_axis(i1, pos // PEER_TOPK, axis=-1) * N_KEYS
             + jnp.take_along_axis(i2, pos % PEER_TOPK, axis=-1))
        gate = jax.nn.softmax(top, axis=-1)
        act = jnp.einsum('thkd,td->thk', u_tab[e], xb).astype(jnp.float32)
        hid = (gate * jax.nn.gelu(act, approximate=False)).astype(xb.dtype)
        return jnp.einsum('thk,thkd->td', hid, v_tab[e])

    return lax.map(block, xt).reshape(b, s, d)


def setup_inputs(seed: int = 0) -> dict:
    key = jax.random.key(seed)
    ks = jax.random.split(key, 20)
    nrm = jax.random.normal
    D = D_MODEL
    return {
        'x': nrm(ks[0], (BATCH, SEQ, D), jnp.float32),
        'p': nrm(ks[1], (DEPTH, BATCH, SEQ, PLE_DIM), jnp.float32),
        'norm_mix': 1.0 + 0.02 * nrm(ks[2], (DEPTH, D), jnp.float32),
        'w_in': nrm(ks[3], (DEPTH, D, IN_W), jnp.float32) * D ** -0.5,
        'gla_gate_w2': nrm(ks[4], (DEPTH, GLA_RANK, GLA_QK_W), jnp.float32) * GLA_RANK ** -0.5,
        'gla_gate_b': 0.1 * nrm(ks[5], (DEPTH, GLA_QK_W), jnp.float32),
        'gla_head_norm': 1.0 + 0.02 * nrm(ks[6], (DEPTH, GLA_DV), jnp.float32),
        'rel_bias': 0.1 * nrm(ks[7], (DEPTH, ATT_HEADS, 2 * REL_CLIP + 1), jnp.float32),
        'w_out': nrm(ks[8], (DEPTH, MIX_W, D), jnp.float32) * MIX_W ** -0.5,
        'norm_ffn': 1.0 + 0.02 * nrm(ks[9], (DEPTH, D), jnp.float32),
        'peer_wq': nrm(ks[10], (DEPTH, D, PEER_HEADS * PEER_DK), jnp.float32) * D ** -0.5,
        'peer_subkeys': nrm(ks[11], (DEPTH, PEER_HEADS, 2, N_KEYS, PEER_DK // 2), jnp.float32) * (PEER_DK // 2) ** -0.5,
        'peer_u': nrm(ks[12], (DEPTH, N_EXPERTS, D), jnp.float32) * D ** -0.5,
        'peer_v': nrm(ks[13], (DEPTH, N_EXPERTS, D), jnp.float32) * PEER_HEADS ** -0.5,
        'norm_ple': 1.0 + 0.02 * nrm(ks[14], (DEPTH, D), jnp.float32),
        'w_ple_gate': nrm(ks[15], (DEPTH, D, D), jnp.float32) * D ** -0.5,
        'w_ple': nrm(ks[16], (DEPTH, PLE_DIM, D), jnp.float32) * PLE_DIM ** -0.5,
        'final_norm': 1.0 + 0.02 * nrm(ks[17], (D,), jnp.float32),
    }


def reference(x, p, norm_mix, w_in, gla_gate_w2, gla_gate_b, gla_head_norm, rel_bias, w_out,
              norm_ffn, peer_wq, peer_subkeys, peer_u, peer_v, norm_ple, w_ple_gate, w_ple, final_norm):
    h = x
    for i in range(DEPTH):
        hn = rms_norm(h, norm_mix[i])
        proj = hn @ w_in[i]
        gq, gk, gv, glr, gog, aq, ak, av = jnp.split(proj, SPLIT_POINTS, axis=-1)
        o_gla = gla_group(gq, gk, gv, glr, gla_gate_w2[i], gla_gate_b[i], gla_head_norm[i], gog)
        o_att = band_attention(aq, ak, av, rel_bias[i])
        h = h + jnp.concatenate([o_gla, o_att], axis=-1) @ w_out[i]
        h = h + peer_ffn(rms_norm(h, norm_ffn[i]), peer_wq[i], peer_subkeys[i], peer_u[i], peer_v[i])
        gate = jax.nn.sigmoid(rms_norm(h, norm_ple[i]) @ w_ple_gate[i])
        h = h + (p[i] @ w_ple[i]) * gate
    return rms_norm(h, final_norm)
```

```python
import functools

import jax
import jax.numpy as jnp
from jax import lax
from jax.experimental import pallas as pl
from jax.experimental.pallas import tpu as pltpu

F32 = jnp.float32
BF16 = jnp.bfloat16
I32 = jnp.int32

RMS_EPS = 1e-6
CHUNK = 64
GLA_HEADS = 4
GLA_DK = 64
GLA_DV = 128
GLA_RANK = 16
GLA_GATE_NORM = 16.0
ATT_HEADS = 8
ATT_DH = 64
BAND_CHUNKS = 8
BAND = (BAND_CHUNKS + 1) * CHUNK
REL_CLIP = 128
PEER_HEADS = 8
N_KEYS = 128
PEER_TOPK = 16
PEER_SLOTS = PEER_HEADS * PEER_TOPK

LANES = 128
SUBLANES = 8
VMEM_LIMIT_BYTES = 56 * 1024 * 1024

HALF_EXPERTS = N_KEYS * N_KEYS // 2
HI_MASK = -65536

NT_DIMS = (((1,), (1,)), ((), ()))


def _params(*sem):
    return pltpu.CompilerParams(dimension_semantics=sem, vmem_limit_bytes=VMEM_LIMIT_BYTES)


def _resident(shape):
    nd = len(shape)
    return pl.BlockSpec(shape, lambda *_: (0,) * nd, pipeline_mode=pl.Buffered(1))


def _rms(x, g):
    ms = jnp.mean(x * x, axis=-1, keepdims=True)
    return x * lax.rsqrt(ms + RMS_EPS) * g


def _sigmoid(x):
    return 1.0 / (1.0 + jnp.exp(-x))


def _dot(a, b):
    return jnp.dot(a, b, preferred_element_type=F32)


def _in_proj_kernel(x_ref, g_ref, wg_ref, wlr_ref, gw2_ref, gb_ref, wa_ref,
                    q_ref, k_ref, lg_ref, v_ref, og_ref, aq_ref, ak_ref, av_ref):
    hn = _rms(x_ref[...], g_ref[...]).astype(BF16)
    gla = _dot(hn, wg_ref[...])
    qk_w = GLA_HEADS * GLA_DK
    v_w = GLA_HEADS * GLA_DV
    q_ref[...] = gla[:, 0:qk_w]
    k_ref[...] = gla[:, qk_w:2 * qk_w]
    v_ref[...] = gla[:, 2 * qk_w:2 * qk_w + v_w]
    og_ref[...] = gla[:, 2 * qk_w + v_w:2 * qk_w + 2 * v_w]
    lr = _dot(hn, wlr_ref[...])
    graw = _dot(lr.astype(BF16), gw2_ref[...]) + gb_ref[...]
    lg_ref[...] = (jnp.minimum(graw, 0.0) - jnp.log(1.0 + jnp.exp(-jnp.abs(graw)))) * (1.0 / GLA_GATE_NORM)
    att = _dot(hn, wa_ref[...])
    a_w = ATT_HEADS * ATT_DH
    aq_ref[...] = att[:, 0:a_w].astype(BF16)
    ak_ref[...] = att[:, a_w:2 * a_w].astype(BF16)
    av_ref[...] = att[:, 2 * a_w:3 * a_w].astype(BF16)


def _in_proj(x, g, wg, wlr, gw2, gb, wa, tm):
    t, d = x.shape
    qk_w, v_w, a_w = GLA_HEADS * GLA_DK, GLA_HEADS * GLA_DV, ATT_HEADS * ATT_DH
    row = lambda w: pl.BlockSpec((tm, w), lambda i: (i, 0))
    outs = [(qk_w, F32), (qk_w, F32), (qk_w, F32), (v_w, F32), (v_w, F32), (a_w, BF16), (a_w, BF16), (a_w, BF16)]
    return pl.pallas_call(
        _in_proj_kernel,
        grid=(t // tm,),
        in_specs=[row(d), _resident(g.shape), _resident(wg.shape), _resident(wlr.shape),
                  _resident(gw2.shape), _resident(gb.shape), _resident(wa.shape)],
        out_specs=[row(w) for w, _ in outs],
        out_shape=[jax.ShapeDtypeStruct((t, w), dt) for w, dt in outs],
        compiler_params=_params("parallel"),
        name="in_proj",
    )(x, g, wg, wlr, gw2, gb, wa)


def _gla_kernel(q_ref, k_ref, lg_ref, v_ref, og_ref, hn_ref, o_ref, s_ref):
    c = pl.program_id(1)

    @pl.when(c == 0)
    def _():
        s_ref[...] = jnp.zeros_like(s_ref)

    qk_w, v_w = GLA_HEADS * GLA_DK, GLA_HEADS * GLA_DV
    q, k, g = q_ref[...], k_ref[...], lg_ref[...]
    r_i = lax.broadcasted_iota(I32, (CHUNK, CHUNK), 0)
    c_i = lax.broadcasted_iota(I32, (CHUNK, CHUNK), 1)
    tri = (c_i <= r_i).astype(BF16)
    g_hi = g.astype(BF16)
    g_lo = (g - g_hi.astype(F32)).astype(BF16)
    cum = _dot(tri, g_hi) + _dot(tri, g_lo)
    mid = cum[CHUNK // 2:CHUNK // 2 + 1, :]
    last = cum[CHUNK - 1:CHUNK, :]
    scale = GLA_DK ** -0.5
    qe = q * jnp.exp(cum - mid) * scale
    ke = (k * jnp.exp(mid - cum)).astype(BF16)
    kst = k * jnp.exp(last - cum)
    qc = (q * jnp.exp(cum) * scale).astype(BF16)
    v = v_ref[...].astype(BF16)

    lane_head = lax.broadcasted_iota(I32, (CHUNK, qk_w), 1) // GLA_DK
    q_stack = jnp.concatenate(
        [jnp.where(lane_head == h, qe, 0.0) for h in range(GLA_HEADS)], axis=0).astype(BF16)
    a = lax.dot_general(q_stack, ke, NT_DIMS, preferred_element_type=F32)
    row_in_chunk = lax.broadcasted_iota(I32, (GLA_HEADS * CHUNK, CHUNK), 0) % CHUNK
    col = lax.broadcasted_iota(I32, (GLA_HEADS * CHUNK, CHUNK), 1)
    a = jnp.where(col <= row_in_chunk, a, 0.0).astype(BF16)
    p = _dot(a, v)
    o_intra = jnp.concatenate(
        [p[h * CHUNK:(h + 1) * CHUNK, h * GLA_DV:(h + 1) * GLA_DV] for h in range(GLA_HEADS)], axis=1)

    s = s_ref[...]
    o_inter = _dot(qc, s.astype(BF16))
    kv = _dot(jnp.transpose(kst).astype(BF16), v)
    s_row_head = lax.broadcasted_iota(I32, (qk_w, v_w), 0) // GLA_DK
    s_col_head = lax.broadcasted_iota(I32, (qk_w, v_w), 1) // GLA_DV
    dec = jnp.exp(last)
    dec_col = jnp.transpose(jnp.broadcast_to(dec, (LANES, qk_w)))
    dec_full = jnp.concatenate([dec_col] * (v_w // LANES), axis=1)
    s_ref[...] = s * dec_full + jnp.where(s_row_head == s_col_head, kv, 0.0)

    o = o_intra + o_inter
    og = og_ref[...]
    silu = og * _sigmoid(og)
    outs = []
    for h in range(GLA_HEADS):
        oh = o[:, h * GLA_DV:(h + 1) * GLA_DV]
        outs.append(_rms(oh, hn_ref[...]))
    o_ref[...] = (jnp.concatenate(outs, axis=1) * silu).astype(o_ref.dtype)


def _gla(q, k, lg, v, og, hn):
    b, s, qk_w = q.shape
    v_w = v.shape[-1]
    blk = lambda w: pl.BlockSpec((None, CHUNK, w), lambda i, j: (i, j, 0))
    return pl.pallas_call(
        _gla_kernel,
        grid=(b, s // CHUNK),
        in_specs=[blk(qk_w), blk(qk_w), blk(qk_w), blk(v_w), blk(v_w), _resident(hn.shape)],
        out_specs=blk(v_w),
        out_shape=jax.ShapeDtypeStruct((b, s, v_w), BF16),
        scratch_shapes=[pltpu.VMEM((qk_w, v_w), F32)],
        compiler_params=_params("parallel", "arbitrary"),
        name="gla",
    )(q, k, lg, v, og, hn)


def _band_kernel(q_ref, k_ref, v_ref, bias_ref, o_ref):
    c = pl.program_id(1)
    start = pl.multiple_of(c * CHUNK, CHUNK)
    pad = BAND_CHUNKS * CHUNK
    q = q_ref[...]
    kw = k_ref[pl.ds(start, BAND), :]
    vw = v_ref[pl.ds(start, BAND), :]
    valid = lax.broadcasted_iota(I32, (CHUNK, BAND), 1) >= (pad - c * CHUNK)
    low = lax.broadcasted_iota(I32, (CHUNK, LANES), 1) < ATT_DH
    scale = ATT_DH ** -0.5
    heads_per_tile = LANES // ATT_DH
    outs = []
    for tile in range(ATT_HEADS // heads_per_tile):
        sl = slice(tile * LANES, (tile + 1) * LANES)
        q2, k2, v2 = q[:, sl], kw[:, sl], vw[:, sl]
        o2 = None
        for j in range(heads_per_tile):
            mine = low if j == 0 else jnp.logical_not(low)
            qm = jnp.where(mine, q2, jnp.zeros_like(q2))
            s = lax.dot_general(qm, k2, NT_DIMS, preferred_element_type=F32)
            s = s * scale + bias_ref[tile * heads_per_tile + j]
            s = jnp.where(valid, s, -1e30)
            e = jnp.exp(s - jnp.max(s, axis=-1, keepdims=True))
            den = jnp.sum(e, axis=-1, keepdims=True)
            pv = _dot(e.astype(BF16), v2) / den
            o2 = pv if o2 is None else jnp.where(low, o2, pv)
        outs.append(o2)
    o_ref[...] = jnp.concatenate(outs, axis=1).astype(o_ref.dtype)


def _band(q, kp, vp, bias):
    b, s, w = q.shape
    sp = kp.shape[1]
    return pl.pallas_call(
        _band_kernel,
        grid=(b, s // CHUNK),
        in_specs=[pl.BlockSpec((None, CHUNK, w), lambda i, j: (i, j, 0)),
                  pl.BlockSpec((None, sp, w), lambda i, j: (i, 0, 0)),
                  pl.BlockSpec((None, sp, w), lambda i, j: (i, 0, 0)),
                  _resident(bias.shape)],
        out_specs=pl.BlockSpec((None, CHUNK, w), lambda i, j: (i, j, 0)),
        out_shape=jax.ShapeDtypeStruct((b, s, w), BF16),
        compiler_params=_params("parallel", "arbitrary"),
        name="band",
    )(q, kp, vp, bias)


def _mix_out_kernel(og_ref, oa_ref, h_ref, wo_ref, gf_ref, wq_ref, h2_ref, xn_ref, qp_ref):
    v_w = og_ref.shape[-1]
    o = _dot(og_ref[...], wo_ref[0:v_w, :]) + _dot(oa_ref[...], wo_ref[v_w:, :])
    h2 = h_ref[...] + o
    h2_ref[...] = h2
    xn = _rms(h2, gf_ref[...])
    xn_ref[...] = xn
    qp = _dot(xn.astype(BF16), wq_ref[...])
    for i in range(qp_ref.shape[0]):
        qp_ref[i] = qp[:, i * LANES:(i + 1) * LANES].astype(BF16)


def _mix_out(og, oa, h, wo, gf, wq, tm):
    t, d = h.shape
    n_sub = wq.shape[1] // LANES
    row = lambda w: pl.BlockSpec((tm, w), lambda i: (i, 0))
    return pl.pallas_call(
        _mix_out_kernel,
        grid=(t // tm,),
        in_specs=[row(og.shape[1]), row(oa.shape[1]), row(d), _resident(wo.shape), _resident(gf.shape),
                  _resident(wq.shape)],
        out_specs=[row(d), row(d), pl.BlockSpec((n_sub, tm, LANES), lambda i: (0, i, 0))],
        out_shape=[jax.ShapeDtypeStruct((t, d), F32), jax.ShapeDtypeStruct((t, d), F32),
                   jax.ShapeDtypeStruct((n_sub, t, LANES), BF16)],
        compiler_params=_params("parallel"),
        name="mix_out",
    )(og, oa, h, wo, gf, wq)


_GRID_REGS = (
    ("a", 0, 0, 0, 8), ("a", 0, 8, 0, 8), ("a", 1, 0, 0, 8), ("b", 0, 8, 0, 8),
    ("a", 2, 0, 0, 5), ("a", 3, 0, 0, 4), ("b", 0, 0, 4, 8), ("b", 1, 0, 4, 8), ("b", 2, 0, 4, 5),
)


def _top_rows(s, iota_rows, count):
    n = s.shape[0]
    vals, ids = [], []
    for _ in range(count):
        m = jnp.max(s, axis=0, keepdims=True)
        idx = jnp.min(jnp.where(s == m, iota_rows, float(n)), axis=0, keepdims=True)
        s = jnp.where(iota_rows == idx, -jnp.inf, s)
        vals.append(m)
        ids.append(idx)
    return vals, ids


def _peer_topk_kernel(q_ref, sk_ref, off_ref, sh_ref, gate_ref, s_scr, i_scr, t_scr, e_scr, g_all, o_all, h_all):
    ntok = q_ref.shape[1]
    iota_keys = lax.broadcasted_iota(I32, (N_KEYS, ntok), 0).astype(F32)
    sub = lax.broadcasted_iota(I32, (SUBLANES, ntok), 0)
    subf = sub.astype(F32)

    def head_body(h, carry):
        for half in range(2):
            sc = lax.dot_general(sk_ref[2 * h + half], q_ref[2 * h + half], NT_DIMS,
                                 preferred_element_type=F32)
            vals, ids = _top_rows(sc, iota_keys, PEER_TOPK)
            for kk in range(PEER_TOPK):
                s_scr[half, kk:kk + 1, :] = vals[kk]
                i_scr[half, kk:kk + 1, :] = ids[kk]
        cand, flat, eid = [], [], []
        for axis, fixed, first, lo, hi in _GRID_REGS:
            run = slice(first, first + SUBLANES)
            if axis == "a":
                val = s_scr[0, fixed:fixed + 1, :] + s_scr[1, run, :]
                ee = i_scr[0, fixed:fixed + 1, :] * float(N_KEYS) + i_scr[1, run, :]
                ff = float(fixed * PEER_TOPK + first) + subf
            else:
                val = s_scr[0, run, :] + s_scr[1, fixed:fixed + 1, :]
                ee = i_scr[0, run, :] * float(N_KEYS) + i_scr[1, fixed:fixed + 1, :]
                ff = (float(first) + subf) * float(PEER_TOPK) + float(fixed)
            if (lo, hi) != (0, SUBLANES):
                val = jnp.where((sub >= lo) & (sub < hi), val, -jnp.inf)
            cand.append(val)
            flat.append(ff)
            eid.append(ee)
        big = float(PEER_TOPK * PEER_TOPK)
        for kk in range(PEER_TOPK):
            m = jnp.max(functools.reduce(jnp.maximum, cand), axis=0, keepdims=True)
            fsel = [jnp.where(cv == m, fv, big) for cv, fv in zip(cand, flat)]
            fm = jnp.min(functools.reduce(jnp.minimum, fsel), axis=0, keepdims=True)
            hit = [fv == fm for fv in flat]
            esel = [jnp.where(hv, ev, -1.0) for hv, ev in zip(hit, eid)]
            e = jnp.max(functools.reduce(jnp.maximum, esel), axis=0, keepdims=True)
            cand = [jnp.where(hv, -jnp.inf, cv) for hv, cv in zip(hit, cand)]
            t_scr[kk:kk + 1, :] = m
            e_scr[kk:kk + 1, :] = e
        top = t_scr[...]
        ex = jnp.exp(top - top[0:1, :])
        gate = ex / jnp.sum(ex, axis=0, keepdims=True)
        e = e_scr[...]
        in_low_half = (e >= float(HALF_EXPERTS)).astype(F32)
        rows = pl.ds(pl.multiple_of(h * PEER_TOPK, PEER_TOPK), PEER_TOPK)
        g_all[rows, :] = gate
        o_all[rows, :] = (e - in_low_half * float(HALF_EXPERTS)) * float(SUBLANES)
        h_all[rows, :] = in_low_half * 16.0
        return carry

    lax.fori_loop(0, PEER_HEADS, head_body, 0)
    gate_ref[...] = jnp.transpose(g_all[...])
    off_ref[...] = jnp.transpose(o_all[...]).astype(I32)
    sh_ref[...] = jnp.transpose(h_all[...]).astype(I32)


def _peer_topk(qp, sk):
    n_sub, t, _ = qp.shape
    ntok = LANES
    out = pl.BlockSpec((ntok, PEER_SLOTS), lambda i: (i, 0))
    return pl.pallas_call(
        _peer_topk_kernel,
        grid=(t // ntok,),
        in_specs=[pl.BlockSpec((n_sub, ntok, LANES), lambda i: (0, i, 0)), _resident(sk.shape)],
        out_specs=[out, out, out],
        out_shape=[jax.ShapeDtypeStruct((t, PEER_SLOTS), I32), jax.ShapeDtypeStruct((t, PEER_SLOTS), I32),
                   jax.ShapeDtypeStruct((t, PEER_SLOTS), F32)],
        scratch_shapes=[pltpu.VMEM((2, PEER_TOPK, ntok), F32), pltpu.VMEM((2, PEER_TOPK, ntok), F32),
                        pltpu.VMEM((PEER_TOPK, ntok), F32), pltpu.VMEM((PEER_TOPK, ntok), F32),
                        pltpu.VMEM((PEER_SLOTS, ntok), F32), pltpu.VMEM((PEER_SLOTS, ntok), F32),
                        pltpu.VMEM((PEER_SLOTS, ntok), F32)],
        compiler_params=_params("parallel"),
        name="peer_topk",
    )(qp, sk)


def _expert_rows(tab_ref, off_ref, shm_ref, base, slot):
    o = pl.multiple_of(off_ref[base + slot], SUBLANES)
    w = tab_ref[pl.ds(o, SUBLANES), :]
    w = jnp.left_shift(w, shm_ref[slot:slot + 1, :]) & HI_MASK
    return lax.bitcast_convert_type(w, F32)


def _sublane_fold(a, b, keep, k):
    return jnp.where(keep, a, b) + jnp.where(keep, pltpu.roll(a, SUBLANES - k, 0), pltpu.roll(b, k, 0))


_FOLD_FEED = (0, 4, 2, 6, 1, 5, 3, 7)


def _sublane_sums(regs, sub):
    keep4, keep2, keep1 = sub < 4, (sub % 4) < 2, (sub % 2) < 1
    l1 = [_sublane_fold(regs[2 * i], regs[2 * i + 1], keep4, 4) for i in range(4)]
    l2 = [_sublane_fold(l1[2 * i], l1[2 * i + 1], keep2, 2) for i in range(2)]
    return _sublane_fold(l2[0], l2[1], keep1, 1)


def _peer_u_kernel(off_ref, x_ref, sh_ref, tab_ref, act_ref, shm_ref):
    ntok = sh_ref.shape[0]
    sub = lax.broadcasted_iota(I32, (SUBLANES, LANES), 0)
    ones = jnp.ones((SUBLANES, LANES), BF16)

    def token(t, carry):
        x = x_ref[pl.ds(pl.multiple_of(t * SUBLANES, SUBLANES), SUBLANES), :]
        shm_ref[...] = jnp.transpose(jnp.broadcast_to(sh_ref[pl.ds(t, 1), :], (LANES, PEER_SLOTS)))
        base = t * PEER_SLOTS
        folded = []
        for grp in range(PEER_SLOTS // SUBLANES):
            prods = [None] * SUBLANES
            for r in range(SUBLANES):
                slot = grp * SUBLANES + r
                prods[_FOLD_FEED.index(r)] = _expert_rows(tab_ref, off_ref, shm_ref, base, slot) * x
            folded.append(_sublane_sums(prods, sub))
        part = jnp.concatenate(folded, axis=0)
        hi = part.astype(BF16)
        lo = (part - hi.astype(F32)).astype(BF16)
        act = (lax.dot_general(ones, hi, NT_DIMS, preferred_element_type=F32)
               + lax.dot_general(ones, lo, NT_DIMS, preferred_element_type=F32))
        act_ref[pl.ds(t, 1), :] = act[0:1, :]
        return carry

    lax.fori_loop(0, ntok, token, 0)


def _peer_u(off_flat, x8, sh, tab, ntok):
    t = sh.shape[0]
    return pl.pallas_call(
        _peer_u_kernel,
        grid=(t // ntok,),
        in_specs=[pl.BlockSpec((ntok * PEER_SLOTS,), lambda i: (i,), memory_space=pltpu.SMEM),
                  pl.BlockSpec((ntok * SUBLANES, LANES), lambda i: (i, 0)),
                  pl.BlockSpec((ntok, PEER_SLOTS), lambda i: (i, 0)),
                  _resident(tab.shape)],
        out_specs=pl.BlockSpec((ntok, PEER_SLOTS), lambda i: (i, 0)),
        out_shape=jax.ShapeDtypeStruct((t, PEER_SLOTS), F32),
        scratch_shapes=[pltpu.VMEM((PEER_SLOTS, LANES), I32)],
        compiler_params=_params("arbitrary"),
        name="peer_u",
    )(off_flat, x8, sh, tab)


def _peer_v_kernel(off_ref, act_ref, gate_ref, sh_ref, h_ref, tab_ref, out_ref, shm_ref, hm_ref, hid_ref):
    ntok = sh_ref.shape[0]
    act = act_ref[...]
    gelu = 0.5 * act * (1.0 + lax.erf(act * (2.0 ** -0.5)))
    hid_ref[...] = gate_ref[...] * gelu
    n_acc = 4

    def token(t, carry):
        shm_ref[...] = jnp.transpose(jnp.broadcast_to(sh_ref[pl.ds(t, 1), :], (LANES, PEER_SLOTS)))
        hm_ref[...] = jnp.transpose(jnp.broadcast_to(hid_ref[pl.ds(t, 1), :], (LANES, PEER_SLOTS)))
        base = t * PEER_SLOTS
        accs = [None] * n_acc
        for slot in range(PEER_SLOTS):
            term = _expert_rows(tab_ref, off_ref, shm_ref, base, slot) * hm_ref[slot:slot + 1, :]
            accs[slot % n_acc] = term if accs[slot % n_acc] is None else accs[slot % n_acc] + term
        rows = pl.ds(pl.multiple_of(t * SUBLANES, SUBLANES), SUBLANES)
        out_ref[rows, :] = h_ref[rows, :] + ((accs[0] + accs[1]) + (accs[2] + accs[3]))
        return carry

    lax.fori_loop(0, ntok, token, 0)


def _peer_v(off_flat, act, gate, sh, h8, tab, ntok):
    t = sh.shape[0]
    slots = pl.BlockSpec((ntok, PEER_SLOTS), lambda i: (i, 0))
    rows8 = pl.BlockSpec((ntok * SUBLANES, LANES), lambda i: (i, 0))
    return pl.pallas_call(
        _peer_v_kernel,
        grid=(t // ntok,),
        in_specs=[pl.BlockSpec((ntok * PEER_SLOTS,), lambda i: (i,), memory_space=pltpu.SMEM),
                  slots, slots, slots, rows8, _resident(tab.shape)],
        out_specs=rows8,
        out_shape=jax.ShapeDtypeStruct(h8.shape, F32),
        scratch_shapes=[pltpu.VMEM((PEER_SLOTS, LANES), I32), pltpu.VMEM((PEER_SLOTS, LANES), F32),
                        pltpu.VMEM((ntok, PEER_SLOTS), F32)],
        compiler_params=_params("arbitrary"),
        name="peer_v",
    )(off_flat, act, gate, sh, h8, tab)


def _ple_kernel(h_ref, p_ref, g_ref, wg_ref, wp_ref, fin_ref, out_ref, *, final):
    h = h_ref[...]
    gate = _sigmoid(_dot(_rms(h, g_ref[...]).astype(BF16), wg_ref[...]))
    h2 = h + _dot(p_ref[...].astype(BF16), wp_ref[...]) * gate
    if final:
        h2 = _rms(h2, fin_ref[...])
    out_ref[...] = h2


def _ple(h, p, g, wg, wp, fin, tm, final):
    t, d = h.shape
    row = lambda w: pl.BlockSpec((tm, w), lambda i: (i, 0))
    return pl.pallas_call(
        functools.partial(_ple_kernel, final=final),
        grid=(t // tm,),
        in_specs=[row(d), row(p.shape[1]), _resident(g.shape), _resident(wg.shape), _resident(wp.shape),
                  _resident(fin.shape)],
        out_specs=row(d),
        out_shape=jax.ShapeDtypeStruct((t, d), F32),
        compiler_params=_params("parallel"),
        name="ple",
    )(h, p, g, wg, wp, fin)


def _pack_table(tab):
    e, d = tab.shape
    bits = lax.bitcast_convert_type(tab.astype(BF16), jnp.uint16).astype(jnp.uint32)
    word = (bits[: e // 2] << 16) | bits[e // 2:]
    return lax.bitcast_convert_type(word, I32).reshape(e // 2 * (d // LANES), LANES)


def _rel_bias(rel_table):
    rel = jnp.arange(CHUNK)[:, None] + BAND_CHUNKS * CHUNK - jnp.arange(BAND)[None, :]
    return rel_table.astype(F32)[:, jnp.clip(rel, -REL_CLIP, REL_CLIP) + REL_CLIP]


def kernel(x, p, norm_mix, w_in, gla_gate_w2, gla_gate_b, gla_head_norm, rel_bias, w_out, norm_ffn, peer_wq,
           peer_subkeys, peer_u, peer_v, norm_ple, w_ple_gate, w_ple, final_norm):
    b, s, d = x.shape
    depth = w_in.shape[0]
    t = b * s
    qk_w, v_w, a_w = GLA_HEADS * GLA_DK, GLA_HEADS * GLA_DV, ATT_HEADS * ATT_DH
    tm = 512
    peer_tok = 64
    h = x.reshape(t, d)
    fin = final_norm.reshape(1, d)
    for i in range(depth):
        w = w_in[i]
        c0 = 2 * qk_w + v_w
        wg = jnp.concatenate([w[:, :c0], w[:, c0 + GLA_RANK:c0 + GLA_RANK + v_w]], axis=1).astype(BF16)
        wlr = jnp.pad(w[:, c0:c0 + GLA_RANK], ((0, 0), (0, LANES - GLA_RANK))).astype(BF16)
        wa = w[:, c0 + GLA_RANK + v_w:].astype(BF16)
        gw2 = jnp.pad(gla_gate_w2[i], ((0, LANES - GLA_RANK), (0, 0))).astype(BF16)
        q, k, lg, v, og, aq, ak, av = _in_proj(h, norm_mix[i].reshape(1, d), wg, wlr, gw2,
                                               gla_gate_b[i].reshape(1, qk_w), wa, tm)
        to3 = lambda z: z.reshape(b, s, z.shape[-1])
        o_gla = _gla(to3(q), to3(k), to3(lg), to3(v), to3(og), gla_head_norm[i].reshape(1, GLA_DV))
        front = ((0, 0), (BAND_CHUNKS * CHUNK, 0), (0, 0))
        o_att = _band(to3(aq), jnp.pad(to3(ak), front), jnp.pad(to3(av), front), _rel_bias(rel_bias[i]))
        h, xn, qp = _mix_out(o_gla.reshape(t, v_w), o_att.reshape(t, a_w), h, w_out[i].astype(BF16),
                             norm_ffn[i].reshape(1, d), peer_wq[i].astype(BF16), tm)
        sk = peer_subkeys[i].reshape(PEER_HEADS * 2, N_KEYS, -1).astype(BF16)
        off, sh, gate = _peer_topk(qp, sk)
        off_flat = off.reshape(t * PEER_SLOTS)
        rows8 = (t * (d // LANES), LANES)
        act = _peer_u(off_flat, xn.reshape(rows8), sh, _pack_table(peer_u[i]), peer_tok)
        h = _peer_v(off_flat, act, gate, sh, h.reshape(rows8), _pack_table(peer_v[i]), peer_tok).reshape(t, d)
        h = _ple(h, p[i].reshape(t, -1), norm_ple[i].reshape(1, d), w_ple_gate[i].astype(BF16),
                 w_ple[i].astype(BF16), fin, tm, final=(i == depth - 1))
    return h.reshape(b, s, d)
```

```python
import functools

import jax
import jax.numpy as jnp
from jax import lax
from jax.experimental import pallas as pl
from jax.experimental.pallas import tpu as pltpu

F32 = jnp.float32
BF16 = jnp.bfloat16
I32 = jnp.int32

RMS_EPS = 1e-6
CHUNK = 64
GLA_HEADS = 4
GLA_DK = 64
GLA_DV = 128
GLA_RANK = 16
GLA_GATE_NORM = 16.0
ATT_HEADS = 8
ATT_DH = 64
BAND_CHUNKS = 8
BAND = (BAND_CHUNKS + 1) * CHUNK
REL_CLIP = 128
PEER_HEADS = 8
N_KEYS = 128
PEER_TOPK = 16
PEER_SLOTS = PEER_HEADS * PEER_TOPK

LANES = 128
SUBLANES = 8
VMEM_LIMIT_BYTES = 56 * 1024 * 1024

HALF_EXPERTS = N_KEYS * N_KEYS // 2
HI_MASK = -65536

NT_DIMS = (((1,), (1,)), ((), ()))


def _params(*sem):
    return pltpu.CompilerParams(dimension_semantics=sem, vmem_limit_bytes=VMEM_LIMIT_BYTES)


def _resident(shape):
    nd = len(shape)
    return pl.BlockSpec(shape, lambda *_: (0,) * nd, pipeline_mode=pl.Buffered(1))


def _rms(x, g):
    ms = jnp.mean(x * x, axis=-1, keepdims=True)
    return x * lax.rsqrt(ms + RMS_EPS) * g


def _sigmoid(x):
    return 1.0 / (1.0 + jnp.exp(-x))


def _dot(a, b):
    return jnp.dot(a, b, preferred_element_type=F32)


def _in_proj_kernel(x_ref, g_ref, wg_ref, wlr_ref, gw2_ref, gb_ref, wa_ref,
                    q_ref, k_ref, lg_ref, v_ref, og_ref, aq_ref, ak_ref, av_ref):
    hn = _rms(x_ref[...], g_ref[...]).astype(BF16)
    gla = _dot(hn, wg_ref[...])
    qk_w = GLA_HEADS * GLA_DK
    v_w = GLA_HEADS * GLA_DV
    q_ref[...] = gla[:, 0:qk_w]
    k_ref[...] = gla[:, qk_w:2 * qk_w]
    v_ref[...] = gla[:, 2 * qk_w:2 * qk_w + v_w]
    og_ref[...] = gla[:, 2 * qk_w + v_w:2 * qk_w + 2 * v_w]
    lr = _dot(hn, wlr_ref[...])
    graw = _dot(lr.astype(BF16), gw2_ref[...]) + gb_ref[...]
    lg_ref[...] = (jnp.minimum(graw, 0.0) - jnp.log(1.0 + jnp.exp(-jnp.abs(graw)))) * (1.0 / GLA_GATE_NORM)
    att = _dot(hn, wa_ref[...])
    a_w = ATT_HEADS * ATT_DH
    aq_ref[...] = att[:, 0:a_w].astype(BF16)
    ak_ref[...] = att[:, a_w:2 * a_w].astype(BF16)
    av_ref[...] = att[:, 2 * a_w:3 * a_w].astype(BF16)


def _in_proj(x, g, wg, wlr, gw2, gb, wa, tm):
    t, d = x.shape
    qk_w, v_w, a_w = GLA_HEADS * GLA_DK, GLA_HEADS * GLA_DV, ATT_HEADS * ATT_DH
    row = lambda w: pl.BlockSpec((tm, w), lambda i: (i, 0))
    outs = [(qk_w, F32), (qk_w, F32), (qk_w, F32), (v_w, F32), (v_w, F32), (a_w, BF16), (a_w, BF16), (a_w, BF16)]
    return pl.pallas_call(
        _in_proj_kernel,
        grid=(t // tm,),
        in_specs=[row(d), _resident(g.shape), _resident(wg.shape), _resident(wlr.shape),
                  _resident(gw2.shape), _resident(gb.shape), _resident(wa.shape)],
        out_specs=[row(w) for w, _ in outs],
        out_shape=[jax.ShapeDtypeStruct((t, w), dt) for w, dt in outs],
        compiler_params=_params("parallel"),
        name="in_proj",
    )(x, g, wg, wlr, gw2, gb, wa)


def _gla_kernel(q_ref, k_ref, lg_ref, v_ref, og_ref, hn_ref, o_ref, s_ref):
    c = pl.program_id(1)

    @pl.when(c == 0)
    def _():
        s_ref[...] = jnp.zeros_like(s_ref)

    qk_w, v_w = GLA_HEADS * GLA_DK, GLA_HEADS * GLA_DV
    q, k, g = q_ref[...], k_ref[...], lg_ref[...]
    r_i = lax.broadcasted_iota(I32, (CHUNK, CHUNK), 0)
    c_i = lax.broadcasted_iota(I32, (CHUNK, CHUNK), 1)
    tri = (c_i <= r_i).astype(BF16)
    g_hi = g.astype(BF16)
    g_lo = (g - g_hi.astype(F32)).astype(BF16)
    cum = _dot(tri, g_hi) + _dot(tri, g_lo)
    mid = cum[CHUNK // 2:CHUNK // 2 + 1, :]
    last = cum[CHUNK - 1:CHUNK, :]
    scale = GLA_DK ** -0.5
    qe = q * jnp.exp(cum - mid) * scale
    ke = (k * jnp.exp(mid - cum)).astype(BF16)
    kst = k * jnp.exp(last - cum)
    qc = (q * jnp.exp(cum) * scale).astype(BF16)
    v = v_ref[...].astype(BF16)

    lane_head = lax.broadcasted_iota(I32, (CHUNK, qk_w), 1) // GLA_DK
    q_stack = jnp.concatenate(
        [jnp.where(lane_head == h, qe, 0.0) for h in range(GLA_HEADS)], axis=0).astype(BF16)
    a = lax.dot_general(q_stack, ke, NT_DIMS, preferred_element_type=F32)
    row_in_chunk = lax.broadcasted_iota(I32, (GLA_HEADS * CHUNK, CHUNK), 0) % CHUNK
    col = lax.broadcasted_iota(I32, (GLA_HEADS * CHUNK, CHUNK), 1)
    a = jnp.where(col <= row_in_chunk, a, 0.0).astype(BF16)
    p = _dot(a, v)
    o_intra = jnp.concatenate(
        [p[h * CHUNK:(h + 1) * CHUNK, h * GLA_DV:(h + 1) * GLA_DV] for h in range(GLA_HEADS)], axis=1)

    s = s_ref[...]
    o_inter = _dot(qc, s.astype(BF16))
    kv = _dot(jnp.transpose(kst).astype(BF16), v)
    s_row_head = lax.broadcasted_iota(I32, (qk_w, v_w), 0) // GLA_DK
    s_col_head = lax.broadcasted_iota(I32, (qk_w, v_w), 1) // GLA_DV
    dec = jnp.exp(last)
    dec_col = jnp.transpose(jnp.broadcast_to(dec, (LANES, qk_w)))
    dec_full = jnp.concatenate([dec_col] * (v_w // LANES), axis=1)
    s_ref[...] = s * dec_full + jnp.where(s_row_head == s_col_head, kv, 0.0)

    o = o_intra + o_inter
    og = og_ref[...]
    silu = og * _sigmoid(og)
    outs = []
    for h in range(GLA_HEADS):
        oh = o[:, h * GLA_DV:(h + 1) * GLA_DV]
        outs.append(_rms(oh, hn_ref[...]))
    o_ref[...] = (jnp.concatenate(outs, axis=1) * silu).astype(o_ref.dtype)


def _gla(q, k, lg, v, og, hn):
    b, s, qk_w = q.shape
    v_w = v.shape[-1]
    blk = lambda w: pl.BlockSpec((None, CHUNK, w), lambda i, j: (i, j, 0))
    return pl.pallas_call(
        _gla_kernel,
        grid=(b, s // CHUNK),
        in_specs=[blk(qk_w), blk(qk_w), blk(qk_w), blk(v_w), blk(v_w), _resident(hn.shape)],
        out_specs=blk(v_w),
        out_shape=jax.ShapeDtypeStruct((b, s, v_w), BF16),
        scratch_shapes=[pltpu.VMEM((qk_w, v_w), F32)],
        compiler_params=_params("parallel", "arbitrary"),
        name="gla",
    )(q, k, lg, v, og, hn)


def _band_kernel(q_ref, k_ref, v_ref, bias_ref, o_ref):
    c = pl.program_id(1)
    start = pl.multiple_of(c * CHUNK, CHUNK)
    pad = BAND_CHUNKS * CHUNK
    q = q_ref[...]
    kw = k_ref[pl.ds(start, BAND), :]
    vw = v_ref[pl.ds(start, BAND), :]
    valid = lax.broadcasted_iota(I32, (CHUNK, BAND), 1) >= (pad - c * CHUNK)
    low = lax.broadcasted_iota(I32, (CHUNK, LANES), 1) < ATT_DH
    scale = ATT_DH ** -0.5
    heads_per_tile = LANES // ATT_DH
    outs = []
    for tile in range(ATT_HEADS // heads_per_tile):
        sl = slice(tile * LANES, (tile + 1) * LANES)
        q2, k2, v2 = q[:, sl], kw[:, sl], vw[:, sl]
        o2 = None
        for j in range(heads_per_tile):
            mine = low if j == 0 else jnp.logical_not(low)
            qm = jnp.where(mine, q2, jnp.zeros_like(q2))
            s = lax.dot_general(qm, k2, NT_DIMS, preferred_element_type=F32)
            s = s * scale + bias_ref[tile * heads_per_tile + j]
            s = jnp.where(valid, s, -1e30)
            e = jnp.exp(s - jnp.max(s, axis=-1, keepdims=True))
            den = jnp.sum(e, axis=-1, keepdims=True)
            pv = _dot(e.astype(BF16), v2) / den
            o2 = pv if o2 is None else jnp.where(low, o2, pv)
        outs.append(o2)
    o_ref[...] = jnp.concatenate(outs, axis=1).astype(o_ref.dtype)


def _band(q, kp, vp, bias):
    b, s, w = q.shape
    sp = kp.shape[1]
    return pl.pallas_call(
        _band_kernel,
        grid=(b, s // CHUNK),
        in_specs=[pl.BlockSpec((None, CHUNK, w), lambda i, j: (i, j, 0)),
                  pl.BlockSpec((None, sp, w), lambda i, j: (i, 0, 0)),
                  pl.BlockSpec((None, sp, w), lambda i, j: (i, 0, 0)),
                  _resident(bias.shape)],
        out_specs=pl.BlockSpec((None, CHUNK, w), lambda i, j: (i, j, 0)),
        out_shape=jax.ShapeDtypeStruct((b, s, w), BF16),
        compiler_params=_params("parallel", "arbitrary"),
        name="band",
    )(q, kp, vp, bias)


def _mix_out_kernel(og_ref, oa_ref, h_ref, wo_ref, gf_ref, wq_ref, h2_ref, xn_ref, qp_ref):
    v_w = og_ref.shape[-1]
    o = _dot(og_ref[...], wo_ref[0:v_w, :]) + _dot(oa_ref[...], wo_ref[v_w:, :])
    h2 = h_ref[...] + o
    h2_ref[...] = h2
    xn = _rms(h2, gf_ref[...])
    xn_ref[...] = xn
    qp = _dot(xn.astype(BF16), wq_ref[...])
    for i in range(qp_ref.shape[0]):
        qp_ref[i] = qp[:, i * LANES:(i + 1) * LANES].astype(BF16)


def _mix_out(og, oa, h, wo, gf, wq, tm):
    t, d = h.shape
    n_sub = wq.shape[1] // LANES
    row = lambda w: pl.BlockSpec((tm, w), lambda i: (i, 0))
    return pl.pallas_call(
        _mix_out_kernel,
        grid=(t // tm,),
        in_specs=[row(og.shape[1]), row(oa.shape[1]), row(d), _resident(wo.shape), _resident(gf.shape),
                  _resident(wq.shape)],
        out_specs=[row(d), row(d), pl.BlockSpec((n_sub, tm, LANES), lambda i: (0, i, 0))],
        out_shape=[jax.ShapeDtypeStruct((t, d), F32), jax.ShapeDtypeStruct((t, d), F32),
                   jax.ShapeDtypeStruct((n_sub, t, LANES), BF16)],
        compiler_params=_params("parallel"),
        name="mix_out",
    )(og, oa, h, wo, gf, wq)


_GRID_REGS = (
    ("a", 0, 0, 0, 8), ("a", 0, 8, 0, 8), ("a", 1, 0, 0, 8), ("b", 0, 8, 0, 8),
    ("a", 2, 0, 0, 5), ("a", 3, 0, 0, 4), ("b", 0, 0, 4, 8), ("b", 1, 0, 4, 8), ("b", 2, 0, 4, 5),
)


def _top_rows(s, iota_rows, count):
    n = s.shape[0]
    vals, ids = [], []
    for _ in range(count):
        m = jnp.max(s, axis=0, keepdims=True)
        idx = jnp.min(jnp.where(s == m, iota_rows, float(n)), axis=0, keepdims=True)
        s = jnp.where(iota_rows == idx, -jnp.inf, s)
        vals.append(m)
        ids.append(idx)
    return vals, ids


def _peer_topk_kernel(q_ref, sk_ref, off_ref, sh_ref, gate_ref, s_scr, i_scr, t_scr, e_scr, g_all, o_all, h_all):
    ntok = q_ref.shape[1]
    iota_keys = lax.broadcasted_iota(I32, (N_KEYS, ntok), 0).astype(F32)
    sub = lax.broadcasted_iota(I32, (SUBLANES, ntok), 0)
    subf = sub.astype(F32)

    def head_body(h, carry):
        for half in range(2):
            sc = lax.dot_general(sk_ref[2 * h + half], q_ref[2 * h + half], NT_DIMS,
                                 preferred_element_type=F32)
            vals, ids = _top_rows(sc, iota_keys, PEER_TOPK)
            for kk in range(PEER_TOPK):
                s_scr[half, kk:kk + 1, :] = vals[kk]
                i_scr[half, kk:kk + 1, :] = ids[kk]
        cand, flat, eid = [], [], []
        for axis, fixed, first, lo, hi in _GRID_REGS:
            run = slice(first, first + SUBLANES)
            if axis == "a":
                val = s_scr[0, fixed:fixed + 1, :] + s_scr[1, run, :]
                ee = i_scr[0, fixed:fixed + 1, :] * float(N_KEYS) + i_scr[1, run, :]
                ff = float(fixed * PEER_TOPK + first) + subf
            else:
                val = s_scr[0, run, :] + s_scr[1, fixed:fixed + 1, :]
                ee = i_scr[0, run, :] * float(N_KEYS) + i_scr[1, fixed:fixed + 1, :]
                ff = (float(first) + subf) * float(PEER_TOPK) + float(fixed)
            if (lo, hi) != (0, SUBLANES):
                val = jnp.where((sub >= lo) & (sub < hi), val, -jnp.inf)
            cand.append(val)
            flat.append(ff)
            eid.append(ee)
        big = float(PEER_TOPK * PEER_TOPK)
        for kk in range(PEER_TOPK):
            m = jnp.max(functools.reduce(jnp.maximum, cand), axis=0, keepdims=True)
            fsel = [jnp.where(cv == m, fv, big) for cv, fv in zip(cand, flat)]
            fm = jnp.min(functools.reduce(jnp.minimum, fsel), axis=0, keepdims=True)
            hit = [fv == fm for fv in flat]
            esel = [jnp.where(hv, ev, -1.0) for hv, ev in zip(hit, eid)]
            e = jnp.max(functools.reduce(jnp.maximum, esel), axis=0, keepdims=True)
            cand = [jnp.where(hv, -jnp.inf, cv) for hv, cv in zip(hit, cand)]
            t_scr[kk:kk + 1, :] = m
            e_scr[kk:kk + 1, :] = e
        top = t_scr[...]
        ex = jnp.exp(top - top[0:1, :])
        gate = ex / jnp.sum(ex, axis=0, keepdims=True)
        e = e_scr[...]
        in_low_half = (e >= float(HALF_EXPERTS)).astype(F32)
        rows = pl.ds(pl.multiple_of(h * PEER_TOPK, PEER_TOPK), PEER_TOPK)
        g_all[rows, :] = gate
        o_all[rows, :] = (e - in_low_half * float(HALF_EXPERTS)) * float(SUBLANES)
        h_all[rows, :] = in_low_half * 16.0
        return carry

    lax.fori_loop(0, PEER_HEADS, head_body, 0)
    gate_ref[...] = jnp.transpose(g_all[...])
    off_ref[...] = jnp.transpose(o_all[...]).astype(I32)
    sh_ref[...] = jnp.transpose(h_all[...]).astype(I32)


def _peer_topk(qp, sk):
    n_sub, t, _ = qp.shape
    ntok = LANES
    out = pl.BlockSpec((ntok, PEER_SLOTS), lambda i: (i, 0))
    return pl.pallas_call(
        _peer_topk_kernel,
        grid=(t // ntok,),
        in_specs=[pl.BlockSpec((n_sub, ntok, LANES), lambda i: (0, i, 0)), _resident(sk.shape)],
        out_specs=[out, out, out],
        out_shape=[jax.ShapeDtypeStruct((t, PEER_SLOTS), I32), jax.ShapeDtypeStruct((t, PEER_SLOTS), I32),
                   jax.ShapeDtypeStruct((t, PEER_SLOTS), F32)],
        scratch_shapes=[pltpu.VMEM((2, PEER_TOPK, ntok), F32), pltpu.VMEM((2, PEER_TOPK, ntok), F32),
                        pltpu.VMEM((PEER_TOPK, ntok), F32), pltpu.VMEM((PEER_TOPK, ntok), F32),
                        pltpu.VMEM((PEER_SLOTS, ntok), F32), pltpu.VMEM((PEER_SLOTS, ntok), F32),
                        pltpu.VMEM((PEER_SLOTS, ntok), F32)],
        compiler_params=_params("parallel"),
        name="peer_topk",
    )(qp, sk)


def _expert_rows(tab_ref, off_ref, shm_ref, tok, j, slot):
    o = pl.multiple_of(off_ref[tok, slot], SUBLANES)
    w = tab_ref[pl.ds(o, SUBLANES), :]
    w = jnp.left_shift(w, shm_ref[j, slot:slot + 1, :]) & HI_MASK
    return lax.bitcast_convert_type(w, F32)


def _lane_broadcast_rows(rows, j):
    return jnp.transpose(jnp.broadcast_to(rows[j:j + 1, :], (LANES, rows.shape[1])))


def _sublane_fold(a, b, keep, k):
    if 2 * k == SUBLANES:
        return jnp.where(keep, a, b) + pltpu.roll(jnp.where(keep, b, a), k, 0)
    return jnp.where(keep, a, b) + jnp.where(keep, pltpu.roll(a, SUBLANES - k, 0), pltpu.roll(b, k, 0))


_FOLD_FEED = (0, 4, 2, 6, 1, 5, 3, 7)
TOKEN_GROUP = SUBLANES


def _sublane_sums(regs, sub):
    keep4, keep2, keep1 = sub < 4, (sub % 4) < 2, (sub % 2) < 1
    l1 = [_sublane_fold(regs[2 * i], regs[2 * i + 1], keep4, 4) for i in range(4)]
    l2 = [_sublane_fold(l1[2 * i], l1[2 * i + 1], keep2, 2) for i in range(2)]
    return _sublane_fold(l2[0], l2[1], keep1, 1)


def _peer_u_kernel(off_ref, x_ref, sh_ref, tab_ref, act_ref, shm_ref, part_ref):
    ntok = sh_ref.shape[0]
    sub = lax.broadcasted_iota(I32, (SUBLANES, LANES), 0)
    pick = (lax.broadcasted_iota(I32, (TOKEN_GROUP, TOKEN_GROUP * LANES), 1) // LANES
            == lax.broadcasted_iota(I32, (TOKEN_GROUP, TOKEN_GROUP * LANES), 0)).astype(BF16)

    def group(g, carry):
        t0 = pl.multiple_of(g * TOKEN_GROUP, TOKEN_GROUP)
        sh_rows = sh_ref[pl.ds(t0, TOKEN_GROUP), :]
        for j in range(TOKEN_GROUP):
            tok = t0 + j
            x = x_ref[pl.ds(pl.multiple_of(tok * SUBLANES, SUBLANES), SUBLANES), :]
            shm_ref[j] = _lane_broadcast_rows(sh_rows, j)
            for grp in range(PEER_SLOTS // SUBLANES):
                prods = [None] * SUBLANES
                for r in range(SUBLANES):
                    slot = grp * SUBLANES + r
                    prods[_FOLD_FEED.index(r)] = _expert_rows(tab_ref, off_ref, shm_ref, tok, j, slot) * x
                part_ref[grp * SUBLANES:(grp + 1) * SUBLANES, j * LANES:(j + 1) * LANES] = _sublane_sums(prods, sub)
        part = part_ref[...]
        hi = part.astype(BF16)
        lo = (part - hi.astype(F32)).astype(BF16)
        act_ref[pl.ds(t0, TOKEN_GROUP), :] = (lax.dot_general(pick, hi, NT_DIMS, preferred_element_type=F32)
                                              + lax.dot_general(pick, lo, NT_DIMS, preferred_element_type=F32))
        return carry

    lax.fori_loop(0, ntok // TOKEN_GROUP, group, 0)


def _peer_u(off, x8, sh, tab, ntok):
    t = sh.shape[0]
    slots = pl.BlockSpec((ntok, PEER_SLOTS), lambda i: (i, 0))
    return pl.pallas_call(
        _peer_u_kernel,
        grid=(t // ntok,),
        in_specs=[pl.BlockSpec((ntok, PEER_SLOTS), lambda i: (i, 0), memory_space=pltpu.SMEM),
                  pl.BlockSpec((ntok * SUBLANES, LANES), lambda i: (i, 0)),
                  slots, _resident(tab.shape)],
        out_specs=slots,
        out_shape=jax.ShapeDtypeStruct((t, PEER_SLOTS), F32),
        scratch_shapes=[pltpu.VMEM((TOKEN_GROUP, PEER_SLOTS, LANES), I32),
                        pltpu.VMEM((PEER_SLOTS, TOKEN_GROUP * LANES), F32)],
        compiler_params=_params("arbitrary"),
        name="peer_u",
    )(off, x8, sh, tab)


def _peer_v_kernel(off_ref, act_ref, gate_ref, sh_ref, h_ref, tab_ref, out_ref, shm_ref, hm_ref, hid_ref):
    ntok = sh_ref.shape[0]
    act = act_ref[...]
    gelu = 0.5 * act * (1.0 + lax.erf(act * (2.0 ** -0.5)))
    hid_ref[...] = gate_ref[...] * gelu
    n_acc = 4

    def group(g, carry):
        t0 = pl.multiple_of(g * TOKEN_GROUP, TOKEN_GROUP)
        sh_rows = sh_ref[pl.ds(t0, TOKEN_GROUP), :]
        hid_rows = hid_ref[pl.ds(t0, TOKEN_GROUP), :]
        for j in range(TOKEN_GROUP):
            tok = t0 + j
            shm_ref[j] = _lane_broadcast_rows(sh_rows, j)
            hm_ref[j] = _lane_broadcast_rows(hid_rows, j)
            accs = [None] * n_acc
            for slot in range(PEER_SLOTS):
                term = _expert_rows(tab_ref, off_ref, shm_ref, tok, j, slot) * hm_ref[j, slot:slot + 1, :]
                accs[slot % n_acc] = term if accs[slot % n_acc] is None else accs[slot % n_acc] + term
            rows = pl.ds(pl.multiple_of(tok * SUBLANES, SUBLANES), SUBLANES)
            out_ref[rows, :] = h_ref[rows, :] + ((accs[0] + accs[1]) + (accs[2] + accs[3]))
        return carry

    lax.fori_loop(0, ntok // TOKEN_GROUP, group, 0)


def _peer_v(off, act, gate, sh, h8, tab, ntok):
    t = sh.shape[0]
    slots = pl.BlockSpec((ntok, PEER_SLOTS), lambda i: (i, 0))
    rows8 = pl.BlockSpec((ntok * SUBLANES, LANES), lambda i: (i, 0))
    return pl.pallas_call(
        _peer_v_kernel,
        grid=(t // ntok,),
        in_specs=[pl.BlockSpec((ntok, PEER_SLOTS), lambda i: (i, 0), memory_space=pltpu.SMEM),
                  slots, slots, slots, rows8, _resident(tab.shape)],
        out_specs=rows8,
        out_shape=jax.ShapeDtypeStruct(h8.shape, F32),
        scratch_shapes=[pltpu.VMEM((TOKEN_GROUP, PEER_SLOTS, LANES), I32),
                        pltpu.VMEM((TOKEN_GROUP, PEER_SLOTS, LANES), F32),
                        pltpu.VMEM((ntok, PEER_SLOTS), F32)],
        compiler_params=_params("arbitrary"),
        name="peer_v",
    )(off, act, gate, sh, h8, tab)


def _ple_kernel(h_ref, p_ref, g_ref, wg_ref, wp_ref, fin_ref, out_ref, *, final):
    h = h_ref[...]
    gate = _sigmoid(_dot(_rms(h, g_ref[...]).astype(BF16), wg_ref[...]))
    h2 = h + _dot(p_ref[...].astype(BF16), wp_ref[...]) * gate
    if final:
        h2 = _rms(h2, fin_ref[...])
    out_ref[...] = h2


def _ple(h, p, g, wg, wp, fin, tm, final):
    t, d = h.shape
    row = lambda w: pl.BlockSpec((tm, w), lambda i: (i, 0))
    return pl.pallas_call(
        functools.partial(_ple_kernel, final=final),
        grid=(t // tm,),
        in_specs=[row(d), row(p.shape[1]), _resident(g.shape), _resident(wg.shape), _resident(wp.shape),
                  _resident(fin.shape)],
        out_specs=row(d),
        out_shape=jax.ShapeDtypeStruct((t, d), F32),
        compiler_params=_params("parallel"),
        name="ple",
    )(h, p, g, wg, wp, fin)


def _pack_table(tab):
    e, d = tab.shape
    bits = lax.bitcast_convert_type(tab.astype(BF16), jnp.uint16).astype(jnp.uint32)
    word = (bits[: e // 2] << 16) | bits[e // 2:]
    return lax.bitcast_convert_type(word, I32).reshape(e // 2 * (d // LANES), LANES)


def _rel_bias(rel_table):
    rel = jnp.arange(CHUNK)[:, None] + BAND_CHUNKS * CHUNK - jnp.arange(BAND)[None, :]
    return rel_table.astype(F32)[:, jnp.clip(rel, -REL_CLIP, REL_CLIP) + REL_CLIP]


def kernel(x, p, norm_mix, w_in, gla_gate_w2, gla_gate_b, gla_head_norm, rel_bias, w_out, norm_ffn, peer_wq,
           peer_subkeys, peer_u, peer_v, norm_ple, w_ple_gate, w_ple, final_norm):
    b, s, d = x.shape
    depth = w_in.shape[0]
    t = b * s
    qk_w, v_w, a_w = GLA_HEADS * GLA_DK, GLA_HEADS * GLA_DV, ATT_HEADS * ATT_DH
    tm = 512
    peer_tok = 64
    h = x.reshape(t, d)
    fin = final_norm.reshape(1, d)
    for i in range(depth):
        w = w_in[i]
        c0 = 2 * qk_w + v_w
        wg = jnp.concatenate([w[:, :c0], w[:, c0 + GLA_RANK:c0 + GLA_RANK + v_w]], axis=1).astype(BF16)
        wlr = jnp.pad(w[:, c0:c0 + GLA_RANK], ((0, 0), (0, LANES - GLA_RANK))).astype(BF16)
        wa = w[:, c0 + GLA_RANK + v_w:].astype(BF16)
        gw2 = jnp.pad(gla_gate_w2[i], ((0, LANES - GLA_RANK), (0, 0))).astype(BF16)
        q, k, lg, v, og, aq, ak, av = _in_proj(h, norm_mix[i].reshape(1, d), wg, wlr, gw2,
                                               gla_gate_b[i].reshape(1, qk_w), wa, tm)
        to3 = lambda z: z.reshape(b, s, z.shape[-1])
        o_gla = _gla(to3(q), to3(k), to3(lg), to3(v), to3(og), gla_head_norm[i].reshape(1, GLA_DV))
        front = ((0, 0), (BAND_CHUNKS * CHUNK, 0), (0, 0))
        o_att = _band(to3(aq), jnp.pad(to3(ak), front), jnp.pad(to3(av), front), _rel_bias(rel_bias[i]))
        h, xn, qp = _mix_out(o_gla.reshape(t, v_w), o_att.reshape(t, a_w), h, w_out[i].astype(BF16),
                             norm_ffn[i].reshape(1, d), peer_wq[i].astype(BF16), tm)
        sk = peer_subkeys[i].reshape(PEER_HEADS * 2, N_KEYS, -1).astype(BF16)
        off, sh, gate = _peer_topk(qp, sk)
        rows8 = (t * (d // LANES), LANES)
        act = _peer_u(off, xn.reshape(rows8), sh, _pack_table(peer_u[i]), peer_tok)
        h = _peer_v(off, act, gate, sh, h.reshape(rows8), _pack_table(peer_v[i]), peer_tok).reshape(t, d)
        h = _ple(h, p[i].reshape(t, -1), norm_ple[i].reshape(1, d), w_ple_gate[i].astype(BF16),
                 w_ple[i].astype(BF16), fin, tm, final=(i == depth - 1))
    return h.reshape(b, s, d)
```

```python
import functools

import jax
import jax.numpy as jnp
from jax import lax
from jax.experimental import pallas as pl
from jax.experimental.pallas import tpu as pltpu

F32 = jnp.float32
BF16 = jnp.bfloat16
I32 = jnp.int32

RMS_EPS = 1e-6
CHUNK = 64
GLA_HEADS = 4
GLA_DK = 64
GLA_DV = 128
GLA_RANK = 16
GLA_GATE_NORM = 16.0
ATT_HEADS = 8
ATT_DH = 64
BAND_CHUNKS = 8
BAND = (BAND_CHUNKS + 1) * CHUNK
REL_CLIP = 128
PEER_HEADS = 8
N_KEYS = 128
PEER_TOPK = 16
PEER_SLOTS = PEER_HEADS * PEER_TOPK
PEER_BLOCK = 128

LANES = 128
SUBLANES = 8
VMEM_LIMIT_BYTES = 56 * 1024 * 1024

HALF_EXPERTS = N_KEYS * N_KEYS // 2
HI_MASK = -65536

NT_DIMS = (((1,), (1,)), ((), ()))


def _params(*sem):
    return pltpu.CompilerParams(dimension_semantics=sem, vmem_limit_bytes=VMEM_LIMIT_BYTES)


def _resident(shape):
    nd = len(shape)
    return pl.BlockSpec(shape, lambda *_: (0,) * nd, pipeline_mode=pl.Buffered(1))


def _rms(x, g):
    ms = jnp.mean(x * x, axis=-1, keepdims=True)
    return x * lax.rsqrt(ms + RMS_EPS) * g


def _sigmoid(x):
    return 1.0 / (1.0 + jnp.exp(-x))


def _dot(a, b):
    return jnp.dot(a, b, preferred_element_type=F32)


def _in_proj_kernel(x_ref, g_ref, wg_ref, wlr_ref, gw2_ref, gb_ref, wa_ref,
                    q_ref, k_ref, lg_ref, v_ref, og_ref, aq_ref, ak_ref, av_ref):
    hn = _rms(x_ref[...], g_ref[...]).astype(BF16)
    gla = _dot(hn, wg_ref[...])
    qk_w = GLA_HEADS * GLA_DK
    v_w = GLA_HEADS * GLA_DV
    q_ref[...] = gla[:, 0:qk_w]
    k_ref[...] = gla[:, qk_w:2 * qk_w]
    v_ref[...] = gla[:, 2 * qk_w:2 * qk_w + v_w]
    og_ref[...] = gla[:, 2 * qk_w + v_w:2 * qk_w + 2 * v_w]
    lr = _dot(hn, wlr_ref[...])
    graw = _dot(lr.astype(BF16), gw2_ref[...]) + gb_ref[...]
    lg_ref[...] = (jnp.minimum(graw, 0.0) - jnp.log(1.0 + jnp.exp(-jnp.abs(graw)))) * (1.0 / GLA_GATE_NORM)
    att = _dot(hn, wa_ref[...])
    a_w = ATT_HEADS * ATT_DH
    aq_ref[...] = att[:, 0:a_w].astype(BF16)
    ak_ref[...] = att[:, a_w:2 * a_w].astype(BF16)
    av_ref[...] = att[:, 2 * a_w:3 * a_w].astype(BF16)


def _in_proj(x, g, wg, wlr, gw2, gb, wa, tm):
    t, d = x.shape
    qk_w, v_w, a_w = GLA_HEADS * GLA_DK, GLA_HEADS * GLA_DV, ATT_HEADS * ATT_DH
    row = lambda w: pl.BlockSpec((tm, w), lambda i: (i, 0))
    outs = [(qk_w, F32), (qk_w, F32), (qk_w, F32), (v_w, F32), (v_w, F32), (a_w, BF16), (a_w, BF16), (a_w, BF16)]
    return pl.pallas_call(
        _in_proj_kernel,
        grid=(t // tm,),
        in_specs=[row(d), _resident(g.shape), _resident(wg.shape), _resident(wlr.shape),
                  _resident(gw2.shape), _resident(gb.shape), _resident(wa.shape)],
        out_specs=[row(w) for w, _ in outs],
        out_shape=[jax.ShapeDtypeStruct((t, w), dt) for w, dt in outs],
        compiler_params=_params("parallel"),
        name="in_proj",
    )(x, g, wg, wlr, gw2, gb, wa)


def _gla_kernel(q_ref, k_ref, lg_ref, v_ref, og_ref, hn_ref, o_ref, s_ref):
    c = pl.program_id(1)

    @pl.when(c == 0)
    def _():
        s_ref[...] = jnp.zeros_like(s_ref)

    qk_w, v_w = GLA_HEADS * GLA_DK, GLA_HEADS * GLA_DV
    q, k, g = q_ref[...], k_ref[...], lg_ref[...]
    r_i = lax.broadcasted_iota(I32, (CHUNK, CHUNK), 0)
    c_i = lax.broadcasted_iota(I32, (CHUNK, CHUNK), 1)
    tri = (c_i <= r_i).astype(BF16)
    g_hi = g.astype(BF16)
    g_lo = (g - g_hi.astype(F32)).astype(BF16)
    cum = _dot(tri, g_hi) + _dot(tri, g_lo)
    mid = cum[CHUNK // 2:CHUNK // 2 + 1, :]
    last = cum[CHUNK - 1:CHUNK, :]
    scale = GLA_DK ** -0.5
    qe = q * jnp.exp(cum - mid) * scale
    ke = (k * jnp.exp(mid - cum)).astype(BF16)
    kst = k * jnp.exp(last - cum)
    qc = (q * jnp.exp(cum) * scale).astype(BF16)
    v = v_ref[...].astype(BF16)

    lane_head = lax.broadcasted_iota(I32, (CHUNK, qk_w), 1) // GLA_DK
    q_stack = jnp.concatenate(
        [jnp.where(lane_head == h, qe, 0.0) for h in range(GLA_HEADS)], axis=0).astype(BF16)
    a = lax.dot_general(q_stack, ke, NT_DIMS, preferred_element_type=F32)
    row_in_chunk = lax.broadcasted_iota(I32, (GLA_HEADS * CHUNK, CHUNK), 0) % CHUNK
    col = lax.broadcasted_iota(I32, (GLA_HEADS * CHUNK, CHUNK), 1)
    a = jnp.where(col <= row_in_chunk, a, 0.0).astype(BF16)
    p = _dot(a, v)
    o_intra = jnp.concatenate(
        [p[h * CHUNK:(h + 1) * CHUNK, h * GLA_DV:(h + 1) * GLA_DV] for h in range(GLA_HEADS)], axis=1)

    s = s_ref[...]
    o_inter = _dot(qc, s.astype(BF16))
    kv = _dot(jnp.transpose(kst).astype(BF16), v)
    s_row_head = lax.broadcasted_iota(I32, (qk_w, v_w), 0) // GLA_DK
    s_col_head = lax.broadcasted_iota(I32, (qk_w, v_w), 1) // GLA_DV
    dec = jnp.exp(last)
    dec_col = jnp.transpose(jnp.broadcast_to(dec, (LANES, qk_w)))
    dec_full = jnp.concatenate([dec_col] * (v_w // LANES), axis=1)
    s_ref[...] = s * dec_full + jnp.where(s_row_head == s_col_head, kv, 0.0)

    o = o_intra + o_inter
    og = og_ref[...]
    silu = og * _sigmoid(og)
    outs = []
    for h in range(GLA_HEADS):
        oh = o[:, h * GLA_DV:(h + 1) * GLA_DV]
        outs.append(_rms(oh, hn_ref[...]))
    o_ref[...] = (jnp.concatenate(outs, axis=1) * silu).astype(o_ref.dtype)


def _gla(q, k, lg, v, og, hn):
    b, s, qk_w = q.shape
    v_w = v.shape[-1]
    blk = lambda w: pl.BlockSpec((None, CHUNK, w), lambda i, j: (i, j, 0))
    return pl.pallas_call(
        _gla_kernel,
        grid=(b, s // CHUNK),
        in_specs=[blk(qk_w), blk(qk_w), blk(qk_w), blk(v_w), blk(v_w), _resident(hn.shape)],
        out_specs=blk(v_w),
        out_shape=jax.ShapeDtypeStruct((b, s, v_w), BF16),
        scratch_shapes=[pltpu.VMEM((qk_w, v_w), F32)],
        compiler_params=_params("parallel", "arbitrary"),
        name="gla",
    )(q, k, lg, v, og, hn)


def _band_kernel(q_ref, k_ref, v_ref, bias_ref, o_ref):
    c = pl.program_id(1)
    start = pl.multiple_of(c * CHUNK, CHUNK)
    pad = BAND_CHUNKS * CHUNK
    q = q_ref[...]
    kw = k_ref[pl.ds(start, BAND), :]
    vw = v_ref[pl.ds(start, BAND), :]
    valid = lax.broadcasted_iota(I32, (CHUNK, BAND), 1) >= (pad - c * CHUNK)
    low = lax.broadcasted_iota(I32, (CHUNK, LANES), 1) < ATT_DH
    scale = ATT_DH ** -0.5
    heads_per_tile = LANES // ATT_DH
    outs = []
    for tile in range(ATT_HEADS // heads_per_tile):
        sl = slice(tile * LANES, (tile + 1) * LANES)
        q2, k2, v2 = q[:, sl], kw[:, sl], vw[:, sl]
        o2 = None
        for j in range(heads_per_tile):
            mine = low if j == 0 else jnp.logical_not(low)
            qm = jnp.where(mine, q2, jnp.zeros_like(q2))
            s = lax.dot_general(qm, k2, NT_DIMS, preferred_element_type=F32)
            s = s * scale + bias_ref[tile * heads_per_tile + j]
            s = jnp.where(valid, s, -1e30)
            e = jnp.exp(s - jnp.max(s, axis=-1, keepdims=True))
            den = jnp.sum(e, axis=-1, keepdims=True)
            pv = _dot(e.astype(BF16), v2) / den
            o2 = pv if o2 is None else jnp.where(low, o2, pv)
        outs.append(o2)
    o_ref[...] = jnp.concatenate(outs, axis=1).astype(o_ref.dtype)


def _band(q, kp, vp, bias):
    b, s, w = q.shape
    sp = kp.shape[1]
    return pl.pallas_call(
        _band_kernel,
        grid=(b, s // CHUNK),
        in_specs=[pl.BlockSpec((None, CHUNK, w), lambda i, j: (i, j, 0)),
                  pl.BlockSpec((None, sp, w), lambda i, j: (i, 0, 0)),
                  pl.BlockSpec((None, sp, w), lambda i, j: (i, 0, 0)),
                  _resident(bias.shape)],
        out_specs=pl.BlockSpec((None, CHUNK, w), lambda i, j: (i, j, 0)),
        out_shape=jax.ShapeDtypeStruct((b, s, w), BF16),
        compiler_params=_params("parallel", "arbitrary"),
        name="band",
    )(q, kp, vp, bias)


def _mix_out_kernel(og_ref, oa_ref, h_ref, wo_ref, gf_ref, wq_ref, h2_ref, xn_ref, qp_ref):
    v_w = og_ref.shape[-1]
    o = _dot(og_ref[...], wo_ref[0:v_w, :]) + _dot(oa_ref[...], wo_ref[v_w:, :])
    h2 = h_ref[...] + o
    h2_ref[...] = h2
    xn = _rms(h2, gf_ref[...])
    xn_ref[...] = xn
    qp = _dot(xn.astype(BF16), wq_ref[...])
    for i in range(qp_ref.shape[0]):
        qp_ref[i] = qp[:, i * LANES:(i + 1) * LANES].astype(BF16)


def _mix_out(og, oa, h, wo, gf, wq, tm):
    t, d = h.shape
    n_sub = wq.shape[1] // LANES
    row = lambda w: pl.BlockSpec((tm, w), lambda i: (i, 0))
    return pl.pallas_call(
        _mix_out_kernel,
        grid=(t // tm,),
        in_specs=[row(og.shape[1]), row(oa.shape[1]), row(d), _resident(wo.shape), _resident(gf.shape),
                  _resident(wq.shape)],
        out_specs=[row(d), row(d), pl.BlockSpec((n_sub, tm, LANES), lambda i: (0, i, 0))],
        out_shape=[jax.ShapeDtypeStruct((t, d), F32), jax.ShapeDtypeStruct((t, d), F32),
                   jax.ShapeDtypeStruct((n_sub, t, LANES), BF16)],
        compiler_params=_params("parallel"),
        name="mix_out",
    )(og, oa, h, wo, gf, wq)


_GRID_REGS = (
    ("a", 0, 0, 0, 8), ("a", 0, 8, 0, 8), ("a", 1, 0, 0, 8), ("b", 0, 8, 0, 8),
    ("a", 2, 0, 0, 5), ("a", 3, 0, 0, 4), ("b", 0, 0, 4, 8), ("b", 1, 0, 4, 8), ("b", 2, 0, 4, 5),
)


TOPK_HEAD_UNROLL = 8


def _top_rows(s, iota_rows, count):
    n = s.shape[0]
    vals, ids = [], []
    for _ in range(count):
        m = jnp.max(s, axis=0, keepdims=True)
        idx = jnp.min(jnp.where(s == m, iota_rows, float(n)), axis=0, keepdims=True)
        s = jnp.where(iota_rows == idx, -jnp.inf, s)
        vals.append(m)
        ids.append(idx)
    return vals, ids


def _peer_topk_kernel(q_ref, sk_ref, off_ref, sh_ref, gate_ref, s_scr, i_scr, t_scr, e_scr, g_all, o_all, h_all):
    ntok = q_ref.shape[1]
    iota_keys = lax.broadcasted_iota(I32, (N_KEYS, ntok), 0).astype(F32)
    sub = lax.broadcasted_iota(I32, (SUBLANES, ntok), 0)
    subf = sub.astype(F32)

    def one_head(h, u):
        for half in range(2):
            sc = lax.dot_general(sk_ref[2 * h + half], q_ref[2 * h + half], NT_DIMS,
                                 preferred_element_type=F32)
            vals, ids = _top_rows(sc, iota_keys, PEER_TOPK)
            for kk in range(PEER_TOPK):
                s_scr[2 * u + half, kk:kk + 1, :] = vals[kk]
                i_scr[2 * u + half, kk:kk + 1, :] = ids[kk]
        cand, flat, eid = [], [], []
        for axis, fixed, first, lo, hi in _GRID_REGS:
            run = slice(first, first + SUBLANES)
            if axis == "a":
                val = s_scr[2 * u, fixed:fixed + 1, :] + s_scr[2 * u + 1, run, :]
                ee = i_scr[2 * u, fixed:fixed + 1, :] * float(N_KEYS) + i_scr[2 * u + 1, run, :]
                ff = float(fixed * PEER_TOPK + first) + subf
            else:
                val = s_scr[2 * u, run, :] + s_scr[2 * u + 1, fixed:fixed + 1, :]
                ee = i_scr[2 * u, run, :] * float(N_KEYS) + i_scr[2 * u + 1, fixed:fixed + 1, :]
                ff = (float(first) + subf) * float(PEER_TOPK) + float(fixed)
            if (lo, hi) != (0, SUBLANES):
                val = jnp.where((sub >= lo) & (sub < hi), val, -jnp.inf)
            cand.append(val)
            flat.append(ff)
            eid.append(ee)
        big = float(PEER_TOPK * PEER_TOPK)
        for kk in range(PEER_TOPK):
            m = jnp.max(functools.reduce(jnp.maximum, cand), axis=0, keepdims=True)
            fsel = [jnp.where(cv == m, fv, big) for cv, fv in zip(cand, flat)]
            fm = jnp.min(functools.reduce(jnp.minimum, fsel), axis=0, keepdims=True)
            hit = [fv == fm for fv in flat]
            esel = [jnp.where(hv, ev, -1.0) for hv, ev in zip(hit, eid)]
            e = jnp.max(functools.reduce(jnp.maximum, esel), axis=0, keepdims=True)
            cand = [jnp.where(hv, -jnp.inf, cv) for hv, cv in zip(hit, cand)]
            t_scr[u, kk:kk + 1, :] = m
            e_scr[u, kk:kk + 1, :] = e
        top = t_scr[u]
        ex = jnp.exp(top - top[0:1, :])
        gate = ex / jnp.sum(ex, axis=0, keepdims=True)
        e = e_scr[u]
        in_low_half = (e >= float(HALF_EXPERTS)).astype(F32)
        rows = pl.ds(pl.multiple_of(h * PEER_TOPK, PEER_TOPK), PEER_TOPK)
        g_all[rows, :] = gate
        o_all[rows, :] = (e - in_low_half * float(HALF_EXPERTS)) * float(SUBLANES)
        h_all[rows, :] = in_low_half * 16.0

    def head_group(g, carry):
        for u in range(TOPK_HEAD_UNROLL):
            one_head(g * TOPK_HEAD_UNROLL + u, u)
        return carry

    lax.fori_loop(0, PEER_HEADS // TOPK_HEAD_UNROLL, head_group, 0)
    gate_ref[...] = jnp.transpose(g_all[...])
    off_ref[...] = o_all[...].astype(I32)
    sh_ref[...] = jnp.transpose(h_all[...]).astype(I32)


def _peer_topk(qp, sk):
    n_sub, t, _ = qp.shape
    ntok = PEER_BLOCK
    out = pl.BlockSpec((ntok, PEER_SLOTS), lambda i: (i, 0))
    return pl.pallas_call(
        _peer_topk_kernel,
        grid=(t // ntok,),
        in_specs=[pl.BlockSpec((n_sub, ntok, LANES), lambda i: (0, i, 0)), _resident(sk.shape)],
        out_specs=[pl.BlockSpec((None, PEER_SLOTS, ntok), lambda i: (i, 0, 0)), out, out],
        out_shape=[jax.ShapeDtypeStruct((t // ntok, PEER_SLOTS, ntok), I32),
                   jax.ShapeDtypeStruct((t, PEER_SLOTS), I32), jax.ShapeDtypeStruct((t, PEER_SLOTS), F32)],
        scratch_shapes=[pltpu.VMEM((2 * TOPK_HEAD_UNROLL, PEER_TOPK, ntok), F32),
                        pltpu.VMEM((2 * TOPK_HEAD_UNROLL, PEER_TOPK, ntok), F32),
                        pltpu.VMEM((TOPK_HEAD_UNROLL, PEER_TOPK, ntok), F32),
                        pltpu.VMEM((TOPK_HEAD_UNROLL, PEER_TOPK, ntok), F32),
                        pltpu.VMEM((PEER_SLOTS, ntok), F32), pltpu.VMEM((PEER_SLOTS, ntok), F32),
                        pltpu.VMEM((PEER_SLOTS, ntok), F32)],
        compiler_params=_params("parallel"),
        name="peer_topk",
    )(qp, sk)


TOKEN_GROUP = SUBLANES
BLOCKS_PER_STEP = 2


def _offsets_copy(off_hbm, blk, sm_ref, sem):
    return pltpu.make_async_copy(off_hbm.at[blk], sm_ref, sem)


def _for_each_offset_block(off_hbm, sm_refs, sems, body):
    step, nsteps = pl.program_id(0), pl.num_programs(0)
    first = step * BLOCKS_PER_STEP

    @pl.when(step == 0)
    def _():
        _offsets_copy(off_hbm, 0, sm_refs[0], sems.at[0]).start()

    for b in range(BLOCKS_PER_STEP):
        _offsets_copy(off_hbm, first + b, sm_refs[b], sems.at[b]).wait()
        if b + 1 < BLOCKS_PER_STEP:
            _offsets_copy(off_hbm, first + b + 1, sm_refs[b + 1], sems.at[b + 1]).start()
        else:
            @pl.when(step + 1 < nsteps)
            def _():
                _offsets_copy(off_hbm, first + BLOCKS_PER_STEP, sm_refs[0], sems.at[0]).start()
        body(b, sm_refs[b])


def _expert_rows(tab_ref, sm_ref, shm_ref, tok, j, slot):
    o = pl.multiple_of(sm_ref.at[pl.ds(slot * PEER_BLOCK, PEER_BLOCK)][tok], SUBLANES)
    w = tab_ref[pl.ds(o, SUBLANES), :]
    w = jnp.left_shift(w, shm_ref[j, slot:slot + 1, :]) & HI_MASK
    return lax.bitcast_convert_type(w, F32)


def _lane_broadcast_rows(rows, j):
    return jnp.transpose(jnp.broadcast_to(rows[j:j + 1, :], (LANES, rows.shape[1])))


def _peer_u_kernel(off_hbm, x_ref, sh_ref, pick_ref, fold_ref, tab_ref, act_ref,
                   sm_a, sm_b, sems, shm_a, shm_b, prod_a, prod_b):
    pairs = PEER_BLOCK // (2 * TOKEN_GROUP)

    @pl.when(pl.program_id(0) == 0)
    def _():
        prod_b[...] = jnp.zeros_like(prod_b)

    def block(b, sm_ref):
        base = b * PEER_BLOCK

        def fill(g, shm_ref, prod_ref):
            t0 = pl.multiple_of(g * TOKEN_GROUP, TOKEN_GROUP)
            sh_rows = sh_ref[pl.ds(base + t0, TOKEN_GROUP), :]
            for j in range(TOKEN_GROUP):
                tok = t0 + j
                x = x_ref[pl.ds(pl.multiple_of((base + tok) * SUBLANES, SUBLANES), SUBLANES), :]
                shm_ref[j] = _lane_broadcast_rows(sh_rows, j)
                for slot in range(0, PEER_SLOTS, 2):
                    two = jnp.concatenate([_expert_rows(tab_ref, sm_ref, shm_ref, tok, j, slot) * x,
                                           _expert_rows(tab_ref, sm_ref, shm_ref, tok, j, slot + 1) * x], axis=0)
                    prod_ref[slot * SUBLANES:(slot + 2) * SUBLANES, j * LANES:(j + 1) * LANES] = two.astype(BF16)

        def reduce(g, prod_ref):
            lane_sums = lax.dot_general(pick_ref[...], prod_ref[...], NT_DIMS,
                                        preferred_element_type=F32)
            hi = lane_sums.astype(BF16).astype(F32)
            both = jnp.concatenate([hi, lane_sums - hi], axis=0).astype(BF16)
            res = _dot(both, fold_ref[...])
            rows = pl.ds(pl.multiple_of(base + g * TOKEN_GROUP, TOKEN_GROUP), TOKEN_GROUP)
            act_ref[rows, :] = res[0:TOKEN_GROUP] + res[TOKEN_GROUP:2 * TOKEN_GROUP]

        def pair(i, carry):
            reduce(jnp.maximum(2 * i - 1, 0), prod_b)
            fill(2 * i, shm_a, prod_a)
            reduce(2 * i, prod_a)
            fill(2 * i + 1, shm_b, prod_b)
            return carry

        lax.fori_loop(0, pairs, pair, 0)
        reduce(2 * pairs - 1, prod_b)

    _for_each_offset_block(off_hbm, (sm_a, sm_b), sems, block)


def _gather_scratch():
    sm = pltpu.SMEM((PEER_SLOTS * PEER_BLOCK,), I32)
    return [sm, sm, pltpu.SemaphoreType.DMA((BLOCKS_PER_STEP,))]


def _peer_u(off, x8, sh, tab):
    t = sh.shape[0]
    ntok = BLOCKS_PER_STEP * PEER_BLOCK
    width, rows = TOKEN_GROUP * LANES, PEER_SLOTS * SUBLANES
    pick = (jnp.arange(width)[None, :] // LANES == jnp.arange(TOKEN_GROUP)[:, None]).astype(BF16)
    fold = (jnp.arange(rows)[:, None] // SUBLANES == jnp.arange(PEER_SLOTS)[None, :]).astype(BF16)
    slots = pl.BlockSpec((ntok, PEER_SLOTS), lambda i: (i, 0))
    shm = pltpu.VMEM((TOKEN_GROUP, PEER_SLOTS, LANES), I32)
    prod = pltpu.VMEM((rows, width), BF16)
    return pl.pallas_call(
        _peer_u_kernel,
        grid=(t // ntok,),
        in_specs=[pl.BlockSpec(memory_space=pl.ANY),
                  pl.BlockSpec((ntok * SUBLANES, LANES), lambda i: (i, 0)),
                  slots, _resident(pick.shape), _resident(fold.shape), _resident(tab.shape)],
        out_specs=slots,
        out_shape=jax.ShapeDtypeStruct((t, PEER_SLOTS), F32),
        scratch_shapes=_gather_scratch() + [shm, shm, prod, prod],
        compiler_params=_params("arbitrary"),
        name="peer_u",
    )(off, x8, sh, pick, fold, tab)


def _peer_v_kernel(off_hbm, act_ref, gate_ref, sh_ref, h_ref, tab_ref, out_ref,
                   sm_a, sm_b, sems, shm_ref, hm_ref, hid_ref):
    act = act_ref[...]
    gelu = 0.5 * act * (1.0 + lax.erf(act * (2.0 ** -0.5)))
    hid_ref[...] = gate_ref[...] * gelu
    n_acc = 4

    def block(b, sm_ref):
        base = b * PEER_BLOCK

        def group(g, carry):
            t0 = pl.multiple_of(g * TOKEN_GROUP, TOKEN_GROUP)
            sh_rows = sh_ref[pl.ds(base + t0, TOKEN_GROUP), :]
            hid_rows = hid_ref[pl.ds(base + t0, TOKEN_GROUP), :]
            for j in range(TOKEN_GROUP):
                tok = t0 + j
                shm_ref[j] = _lane_broadcast_rows(sh_rows, j)
                hm_ref[j] = _lane_broadcast_rows(hid_rows, j)
                accs = [None] * n_acc
                for slot in range(PEER_SLOTS):
                    term = _expert_rows(tab_ref, sm_ref, shm_ref, tok, j, slot) * hm_ref[j, slot:slot + 1, :]
                    accs[slot % n_acc] = term if accs[slot % n_acc] is None else accs[slot % n_acc] + term
                rows = pl.ds(pl.multiple_of((base + tok) * SUBLANES, SUBLANES), SUBLANES)
                out_ref[rows, :] = h_ref[rows, :] + ((accs[0] + accs[1]) + (accs[2] + accs[3]))
            return carry

        lax.fori_loop(0, PEER_BLOCK // TOKEN_GROUP, group, 0)

    _for_each_offset_block(off_hbm, (sm_a, sm_b), sems, block)


def _peer_v(off, act, gate, sh, h8, tab):
    t = sh.shape[0]
    ntok = BLOCKS_PER_STEP * PEER_BLOCK
    slots = pl.BlockSpec((ntok, PEER_SLOTS), lambda i: (i, 0))
    rows8 = pl.BlockSpec((ntok * SUBLANES, LANES), lambda i: (i, 0))
    return pl.pallas_call(
        _peer_v_kernel,
        grid=(t // ntok,),
        in_specs=[pl.BlockSpec(memory_space=pl.ANY), slots, slots, slots, rows8, _resident(tab.shape)],
        out_specs=rows8,
        out_shape=jax.ShapeDtypeStruct(h8.shape, F32),
        scratch_shapes=_gather_scratch() + [pltpu.VMEM((TOKEN_GROUP, PEER_SLOTS, LANES), I32),
                                            pltpu.VMEM((TOKEN_GROUP, PEER_SLOTS, LANES), F32),
                                            pltpu.VMEM((ntok, PEER_SLOTS), F32)],
        compiler_params=_params("arbitrary"),
        name="peer_v",
    )(off, act, gate, sh, h8, tab)


def _ple_kernel(h_ref, p_ref, g_ref, wg_ref, wp_ref, fin_ref, out_ref, *, final):
    h = h_ref[...]
    gate = _sigmoid(_dot(_rms(h, g_ref[...]).astype(BF16), wg_ref[...]))
    h2 = h + _dot(p_ref[...].astype(BF16), wp_ref[...]) * gate
    if final:
        h2 = _rms(h2, fin_ref[...])
    out_ref[...] = h2


def _ple(h, p, g, wg, wp, fin, tm, final):
    t, d = h.shape
    row = lambda w: pl.BlockSpec((tm, w), lambda i: (i, 0))
    return pl.pallas_call(
        functools.partial(_ple_kernel, final=final),
        grid=(t // tm,),
        in_specs=[row(d), row(p.shape[1]), _resident(g.shape), _resident(wg.shape), _resident(wp.shape),
                  _resident(fin.shape)],
        out_specs=row(d),
        out_shape=jax.ShapeDtypeStruct((t, d), F32),
        compiler_params=_params("parallel"),
        name="ple",
    )(h, p, g, wg, wp, fin)


def _pack_table(tab):
    e, d = tab.shape
    bits = lax.bitcast_convert_type(tab.astype(BF16), jnp.uint16).astype(jnp.uint32)
    word = (bits[: e // 2] << 16) | bits[e // 2:]
    return lax.bitcast_convert_type(word, I32).reshape(e // 2 * (d // LANES), LANES)


def _rel_bias(rel_table):
    rel = jnp.arange(CHUNK)[:, None] + BAND_CHUNKS * CHUNK - jnp.arange(BAND)[None, :]
    return rel_table.astype(F32)[:, jnp.clip(rel, -REL_CLIP, REL_CLIP) + REL_CLIP]


def kernel(x, p, norm_mix, w_in, gla_gate_w2, gla_gate_b, gla_head_norm, rel_bias, w_out, norm_ffn, peer_wq,
           peer_subkeys, peer_u, peer_v, norm_ple, w_ple_gate, w_ple, final_norm):
    b, s, d = x.shape
    depth = w_in.shape[0]
    t = b * s
    qk_w, v_w, a_w = GLA_HEADS * GLA_DK, GLA_HEADS * GLA_DV, ATT_HEADS * ATT_DH
    tm = 512
    h = x.reshape(t, d)
    fin = final_norm.reshape(1, d)
    for i in range(depth):
        w = w_in[i]
        c0 = 2 * qk_w + v_w
        wg = jnp.concatenate([w[:, :c0], w[:, c0 + GLA_RANK:c0 + GLA_RANK + v_w]], axis=1).astype(BF16)
        wlr = jnp.pad(w[:, c0:c0 + GLA_RANK], ((0, 0), (0, LANES - GLA_RANK))).astype(BF16)
        wa = w[:, c0 + GLA_RANK + v_w:].astype(BF16)
        gw2 = jnp.pad(gla_gate_w2[i], ((0, LANES - GLA_RANK), (0, 0))).astype(BF16)
        q, k, lg, v, og, aq, ak, av = _in_proj(h, norm_mix[i].reshape(1, d), wg, wlr, gw2,
                                               gla_gate_b[i].reshape(1, qk_w), wa, tm)
        to3 = lambda z: z.reshape(b, s, z.shape[-1])
        o_gla = _gla(to3(q), to3(k), to3(lg), to3(v), to3(og), gla_head_norm[i].reshape(1, GLA_DV))
        front = ((0, 0), (BAND_CHUNKS * CHUNK, 0), (0, 0))
        o_att = _band(to3(aq), jnp.pad(to3(ak), front), jnp.pad(to3(av), front), _rel_bias(rel_bias[i]))
        h, xn, qp = _mix_out(o_gla.reshape(t, v_w), o_att.reshape(t, a_w), h, w_out[i].astype(BF16),
                             norm_ffn[i].reshape(1, d), peer_wq[i].astype(BF16), tm)
        sk = peer_subkeys[i].reshape(PEER_HEADS * 2, N_KEYS, -1).astype(BF16)
        off, sh, gate = _peer_topk(qp, sk)
        off = off.reshape(t // PEER_BLOCK, PEER_SLOTS * PEER_BLOCK)
        rows8 = (t * (d // LANES), LANES)
        act = _peer_u(off, xn.reshape(rows8), sh, _pack_table(peer_u[i]))
        h = _peer_v(off, act, gate, sh, h.reshape(rows8), _pack_table(peer_v[i])).reshape(t, d)
        h = _ple(h, p[i].reshape(t, -1), norm_ple[i].reshape(1, d), w_ple_gate[i].astype(BF16),
                 w_ple[i].astype(BF16), fin, tm, final=(i == depth - 1))
    return h.reshape(b, s, d)
```

```python
import functools

import jax
import jax.numpy as jnp
from jax import lax
from jax.experimental import pallas as pl
from jax.experimental.pallas import tpu as pltpu

F32 = jnp.float32
BF16 = jnp.bfloat16
I32 = jnp.int32

RMS_EPS = 1e-6
CHUNK = 64
GLA_HEADS = 4
GLA_DK = 64
GLA_DV = 128
GLA_RANK = 16
GLA_GATE_NORM = 16.0
ATT_HEADS = 8
ATT_DH = 64
BAND_CHUNKS = 8
BAND = (BAND_CHUNKS + 1) * CHUNK
REL_CLIP = 128
PEER_HEADS = 8
N_KEYS = 128
PEER_TOPK = 16
PEER_SLOTS = PEER_HEADS * PEER_TOPK
PEER_BLOCK = 128

LANES = 128
SUBLANES = 8
VMEM_LIMIT_BYTES = 56 * 1024 * 1024

HALF_EXPERTS = N_KEYS * N_KEYS // 2
HI_MASK = -65536

NT_DIMS = (((1,), (1,)), ((), ()))


def _params(*sem):
    return pltpu.CompilerParams(dimension_semantics=sem, vmem_limit_bytes=VMEM_LIMIT_BYTES)


def _resident(shape):
    nd = len(shape)
    return pl.BlockSpec(shape, lambda *_: (0,) * nd, pipeline_mode=pl.Buffered(1))


def _rms(x, g):
    ms = jnp.mean(x * x, axis=-1, keepdims=True)
    return x * lax.rsqrt(ms + RMS_EPS) * g


def _sigmoid(x):
    return 1.0 / (1.0 + jnp.exp(-x))


def _dot(a, b):
    return jnp.dot(a, b, preferred_element_type=F32)


def _in_proj_kernel(x_ref, g_ref, wg_ref, wlr_ref, gw2_ref, gb_ref, wa_ref,
                    q_ref, k_ref, lg_ref, v_ref, og_ref, aq_ref, ak_ref, av_ref):
    hn = _rms(x_ref[...], g_ref[...]).astype(BF16)
    gla = _dot(hn, wg_ref[...])
    qk_w = GLA_HEADS * GLA_DK
    v_w = GLA_HEADS * GLA_DV
    q_ref[...] = gla[:, 0:qk_w]
    k_ref[...] = gla[:, qk_w:2 * qk_w]
    v_ref[...] = gla[:, 2 * qk_w:2 * qk_w + v_w]
    og_ref[...] = gla[:, 2 * qk_w + v_w:2 * qk_w + 2 * v_w]
    lr = _dot(hn, wlr_ref[...])
    graw = _dot(lr.astype(BF16), gw2_ref[...]) + gb_ref[...]
    lg_ref[...] = (jnp.minimum(graw, 0.0) - jnp.log(1.0 + jnp.exp(-jnp.abs(graw)))) * (1.0 / GLA_GATE_NORM)
    att = _dot(hn, wa_ref[...])
    a_w = ATT_HEADS * ATT_DH
    aq_ref[...] = att[:, 0:a_w].astype(BF16)
    ak_ref[...] = att[:, a_w:2 * a_w].astype(BF16)
    av_ref[...] = att[:, 2 * a_w:3 * a_w].astype(BF16)


def _in_proj(x, g, wg, wlr, gw2, gb, wa, tm):
    t, d = x.shape
    qk_w, v_w, a_w = GLA_HEADS * GLA_DK, GLA_HEADS * GLA_DV, ATT_HEADS * ATT_DH
    row = lambda w: pl.BlockSpec((tm, w), lambda i: (i, 0))
    outs = [(qk_w, F32), (qk_w, F32), (qk_w, F32), (v_w, F32), (v_w, F32), (a_w, BF16), (a_w, BF16), (a_w, BF16)]
    return pl.pallas_call(
        _in_proj_kernel,
        grid=(t // tm,),
        in_specs=[row(d), _resident(g.shape), _resident(wg.shape), _resident(wlr.shape),
                  _resident(gw2.shape), _resident(gb.shape), _resident(wa.shape)],
        out_specs=[row(w) for w, _ in outs],
        out_shape=[jax.ShapeDtypeStruct((t, w), dt) for w, dt in outs],
        compiler_params=_params("parallel"),
        name="in_proj",
    )(x, g, wg, wlr, gw2, gb, wa)


def _gla_kernel(q_ref, k_ref, lg_ref, v_ref, og_ref, hn_ref, o_ref, s_ref):
    c = pl.program_id(1)

    @pl.when(c == 0)
    def _():
        s_ref[...] = jnp.zeros_like(s_ref)

    qk_w, v_w = GLA_HEADS * GLA_DK, GLA_HEADS * GLA_DV
    q, k, g = q_ref[...], k_ref[...], lg_ref[...]
    r_i = lax.broadcasted_iota(I32, (CHUNK, CHUNK), 0)
    c_i = lax.broadcasted_iota(I32, (CHUNK, CHUNK), 1)
    tri = (c_i <= r_i).astype(BF16)
    g_hi = g.astype(BF16)
    g_lo = (g - g_hi.astype(F32)).astype(BF16)
    cum = _dot(tri, g_hi) + _dot(tri, g_lo)
    mid = cum[CHUNK // 2:CHUNK // 2 + 1, :]
    last = cum[CHUNK - 1:CHUNK, :]
    scale = GLA_DK ** -0.5
    qe = q * jnp.exp(cum - mid) * scale
    ke = (k * jnp.exp(mid - cum)).astype(BF16)
    kst = k * jnp.exp(last - cum)
    qc = (q * jnp.exp(cum) * scale).astype(BF16)
    v = v_ref[...].astype(BF16)

    lane_head = lax.broadcasted_iota(I32, (CHUNK, qk_w), 1) // GLA_DK
    q_stack = jnp.concatenate(
        [jnp.where(lane_head == h, qe, 0.0) for h in range(GLA_HEADS)], axis=0).astype(BF16)
    a = lax.dot_general(q_stack, ke, NT_DIMS, preferred_element_type=F32)
    row_in_chunk = lax.broadcasted_iota(I32, (GLA_HEADS * CHUNK, CHUNK), 0) % CHUNK
    col = lax.broadcasted_iota(I32, (GLA_HEADS * CHUNK, CHUNK), 1)
    a = jnp.where(col <= row_in_chunk, a, 0.0).astype(BF16)
    p = _dot(a, v)
    o_intra = jnp.concatenate(
        [p[h * CHUNK:(h + 1) * CHUNK, h * GLA_DV:(h + 1) * GLA_DV] for h in range(GLA_HEADS)], axis=1)

    s = s_ref[...]
    o_inter = _dot(qc, s.astype(BF16))
    kv = _dot(jnp.transpose(kst).astype(BF16), v)
    s_row_head = lax.broadcasted_iota(I32, (qk_w, v_w), 0) // GLA_DK
    s_col_head = lax.broadcasted_iota(I32, (qk_w, v_w), 1) // GLA_DV
    dec = jnp.exp(last)
    dec_col = jnp.transpose(jnp.broadcast_to(dec, (LANES, qk_w)))
    dec_full = jnp.concatenate([dec_col] * (v_w // LANES), axis=1)
    s_ref[...] = s * dec_full + jnp.where(s_row_head == s_col_head, kv, 0.0)

    o = o_intra + o_inter
    og = og_ref[...]
    silu = og * _sigmoid(og)
    outs = []
    for h in range(GLA_HEADS):
        oh = o[:, h * GLA_DV:(h + 1) * GLA_DV]
        outs.append(_rms(oh, hn_ref[...]))
    o_ref[...] = (jnp.concatenate(outs, axis=1) * silu).astype(o_ref.dtype)


def _gla(q, k, lg, v, og, hn):
    b, s, qk_w = q.shape
    v_w = v.shape[-1]
    blk = lambda w: pl.BlockSpec((None, CHUNK, w), lambda i, j: (i, j, 0))
    return pl.pallas_call(
        _gla_kernel,
        grid=(b, s // CHUNK),
        in_specs=[blk(qk_w), blk(qk_w), blk(qk_w), blk(v_w), blk(v_w), _resident(hn.shape)],
        out_specs=blk(v_w),
        out_shape=jax.ShapeDtypeStruct((b, s, v_w), BF16),
        scratch_shapes=[pltpu.VMEM((qk_w, v_w), F32)],
        compiler_params=_params("parallel", "arbitrary"),
        name="gla",
    )(q, k, lg, v, og, hn)


BAND_CHUNKS_PER_STEP = 2
HEADS_PER_TILE = LANES // ATT_DH


def _band_kernel(q_ref, k_ref, v_ref, bias_ref, o_ref):
    step = pl.program_id(1)
    pad = BAND_CHUNKS * CHUNK
    rows = HEADS_PER_TILE * CHUNK
    low = lax.broadcasted_iota(I32, (CHUNK, LANES), 1) < ATT_DH
    scale = ATT_DH ** -0.5
    for u in range(BAND_CHUNKS_PER_STEP):
        c = step * BAND_CHUNKS_PER_STEP + u
        start = pl.multiple_of(c * CHUNK, CHUNK)
        q = q_ref[u * CHUNK:(u + 1) * CHUNK, :]
        kw = k_ref[pl.ds(start, BAND), :]
        vw = v_ref[pl.ds(start, BAND), :]
        valid = lax.broadcasted_iota(I32, (rows, BAND), 1) >= (pad - c * CHUNK)
        outs = []
        for tile in range(ATT_HEADS // HEADS_PER_TILE):
            sl = slice(tile * LANES, (tile + 1) * LANES)
            q2, k2, v2 = q[:, sl], kw[:, sl], vw[:, sl]
            zero = jnp.zeros_like(q2)
            q_stack = jnp.concatenate([jnp.where(low, q2, zero), jnp.where(low, zero, q2)], axis=0)
            s = lax.dot_general(q_stack, k2, NT_DIMS, preferred_element_type=F32)
            s = jnp.where(valid, s * scale + bias_ref[tile], -1e30)
            e = jnp.exp(s - jnp.max(s, axis=-1, keepdims=True))
            den = jnp.sum(e, axis=-1, keepdims=True)
            pv = _dot(e.astype(BF16), v2) / den
            outs.append(jnp.where(low, pv[0:CHUNK], pv[CHUNK:2 * CHUNK]))
        o_ref[u * CHUNK:(u + 1) * CHUNK, :] = jnp.concatenate(outs, axis=1).astype(o_ref.dtype)


def _band(q, kp, vp, bias):
    b, s, w = q.shape
    sp = kp.shape[1]
    rows = BAND_CHUNKS_PER_STEP * CHUNK
    return pl.pallas_call(
        _band_kernel,
        grid=(b, s // rows),
        in_specs=[pl.BlockSpec((None, rows, w), lambda i, j: (i, j, 0)),
                  pl.BlockSpec((None, sp, w), lambda i, j: (i, 0, 0)),
                  pl.BlockSpec((None, sp, w), lambda i, j: (i, 0, 0)),
                  _resident(bias.shape)],
        out_specs=pl.BlockSpec((None, rows, w), lambda i, j: (i, j, 0)),
        out_shape=jax.ShapeDtypeStruct((b, s, w), BF16),
        compiler_params=_params("parallel", "arbitrary"),
        name="band",
    )(q, kp, vp, bias)


def _mix_out_kernel(og_ref, oa_ref, h_ref, wo_ref, gf_ref, wq_ref, h2_ref, xn_ref, qp_ref):
    v_w = og_ref.shape[-1]
    o = _dot(og_ref[...], wo_ref[0:v_w, :]) + _dot(oa_ref[...], wo_ref[v_w:, :])
    h2 = h_ref[...] + o
    xn = _rms(h2, gf_ref[...])
    for c in range(h2_ref.shape[1]):
        h2_ref[:, c, :] = h2[:, c * LANES:(c + 1) * LANES]
        xn_ref[:, c, :] = xn[:, c * LANES:(c + 1) * LANES]
    qp = _dot(xn.astype(BF16), wq_ref[...])
    for i in range(qp_ref.shape[0]):
        qp_ref[i] = qp[:, i * LANES:(i + 1) * LANES].astype(BF16)


def _tiled_rows(tm, d):
    return pl.BlockSpec((tm, d // LANES, LANES), lambda i: (i, 0, 0))


def _mix_out(og, oa, h, wo, gf, wq, tm):
    t, d = h.shape
    n_sub = wq.shape[1] // LANES
    row = lambda w: pl.BlockSpec((tm, w), lambda i: (i, 0))
    return pl.pallas_call(
        _mix_out_kernel,
        grid=(t // tm,),
        in_specs=[row(og.shape[1]), row(oa.shape[1]), row(d), _resident(wo.shape), _resident(gf.shape),
                  _resident(wq.shape)],
        out_specs=[_tiled_rows(tm, d), _tiled_rows(tm, d), pl.BlockSpec((n_sub, tm, LANES), lambda i: (0, i, 0))],
        out_shape=[jax.ShapeDtypeStruct((t, d // LANES, LANES), F32),
                   jax.ShapeDtypeStruct((t, d // LANES, LANES), F32),
                   jax.ShapeDtypeStruct((n_sub, t, LANES), BF16)],
        compiler_params=_params("parallel"),
        name="mix_out",
    )(og, oa, h, wo, gf, wq)


_GRID_REGS = (
    ("a", 0, 0, 0, 8), ("a", 0, 8, 0, 8), ("a", 1, 0, 0, 8), ("b", 0, 8, 0, 8),
    ("a", 2, 0, 0, 5), ("a", 3, 0, 0, 4), ("b", 0, 0, 4, 8), ("b", 1, 0, 4, 8), ("b", 2, 0, 4, 5),
)


TOPK_HEAD_UNROLL = 8


def _top_rows(s, iota_rows, count):
    n = s.shape[0]
    vals, ids = [], []
    for _ in range(count):
        m = jnp.max(s, axis=0, keepdims=True)
        idx = jnp.min(jnp.where(s == m, iota_rows, float(n)), axis=0, keepdims=True)
        s = jnp.where(iota_rows == idx, -jnp.inf, s)
        vals.append(m)
        ids.append(idx)
    return vals, ids


def _peer_topk_kernel(q_ref, sk_ref, off_ref, sh_ref, gate_ref, s_scr, i_scr, t_scr, e_scr, g_all, o_all, h_all):
    ntok = q_ref.shape[1]
    iota_keys = lax.broadcasted_iota(I32, (N_KEYS, ntok), 0).astype(F32)
    sub = lax.broadcasted_iota(I32, (SUBLANES, ntok), 0)
    subf = sub.astype(F32)

    def one_head(h, u):
        for half in range(2):
            sc = lax.dot_general(sk_ref[2 * h + half], q_ref[2 * h + half], NT_DIMS,
                                 preferred_element_type=F32)
            vals, ids = _top_rows(sc, iota_keys, PEER_TOPK)
            for kk in range(PEER_TOPK):
                s_scr[2 * u + half, kk:kk + 1, :] = vals[kk]
                i_scr[2 * u + half, kk:kk + 1, :] = ids[kk]
        cand, flat, eid = [], [], []
        for axis, fixed, first, lo, hi in _GRID_REGS:
            run = slice(first, first + SUBLANES)
            if axis == "a":
                val = s_scr[2 * u, fixed:fixed + 1, :] + s_scr[2 * u + 1, run, :]
                ee = i_scr[2 * u, fixed:fixed + 1, :] * float(N_KEYS) + i_scr[2 * u + 1, run, :]
                ff = float(fixed * PEER_TOPK + first) + subf
            else:
                val = s_scr[2 * u, run, :] + s_scr[2 * u + 1, fixed:fixed + 1, :]
                ee = i_scr[2 * u, run, :] * float(N_KEYS) + i_scr[2 * u + 1, fixed:fixed + 1, :]
                ff = (float(first) + subf) * float(PEER_TOPK) + float(fixed)
            if (lo, hi) != (0, SUBLANES):
                val = jnp.where((sub >= lo) & (sub < hi), val, -jnp.inf)
            cand.append(val)
            flat.append(ff)
            eid.append(ee)
        big = float(PEER_TOPK * PEER_TOPK)
        for kk in range(PEER_TOPK):
            m = jnp.max(functools.reduce(jnp.maximum, cand), axis=0, keepdims=True)
            fsel = [jnp.where(cv == m, fv, big) for cv, fv in zip(cand, flat)]
            fm = jnp.min(functools.reduce(jnp.minimum, fsel), axis=0, keepdims=True)
            hit = [fv == fm for fv in flat]
            esel = [jnp.where(hv, ev, -1.0) for hv, ev in zip(hit, eid)]
            e = jnp.max(functools.reduce(jnp.maximum, esel), axis=0, keepdims=True)
            cand = [jnp.where(hv, -jnp.inf, cv) for hv, cv in zip(hit, cand)]
            t_scr[u, kk:kk + 1, :] = m
            e_scr[u, kk:kk + 1, :] = e
        top = t_scr[u]
        ex = jnp.exp(top - top[0:1, :])
        gate = ex / jnp.sum(ex, axis=0, keepdims=True)
        e = e_scr[u]
        in_low_half = (e >= float(HALF_EXPERTS)).astype(F32)
        rows = pl.ds(pl.multiple_of(h * PEER_TOPK, PEER_TOPK), PEER_TOPK)
        g_all[rows, :] = gate
        o_all[rows, :] = (e - in_low_half * float(HALF_EXPERTS)) * float(SUBLANES)
        h_all[rows, :] = in_low_half * 16.0

    def head_group(g, carry):
        for u in range(TOPK_HEAD_UNROLL):
            one_head(g * TOPK_HEAD_UNROLL + u, u)
        return carry

    lax.fori_loop(0, PEER_HEADS // TOPK_HEAD_UNROLL, head_group, 0)
    gate_ref[...] = jnp.transpose(g_all[...])
    off_ref[...] = o_all[...].astype(I32)
    sh_ref[...] = jnp.transpose(h_all[...]).astype(I32)


def _peer_topk(qp, sk):
    n_sub, t, _ = qp.shape
    ntok = PEER_BLOCK
    out = pl.BlockSpec((ntok, PEER_SLOTS), lambda i: (i, 0))
    return pl.pallas_call(
        _peer_topk_kernel,
        grid=(t // ntok,),
        in_specs=[pl.BlockSpec((n_sub, ntok, LANES), lambda i: (0, i, 0)), _resident(sk.shape)],
        out_specs=[pl.BlockSpec((None, PEER_SLOTS, ntok), lambda i: (i, 0, 0)), out, out],
        out_shape=[jax.ShapeDtypeStruct((t // ntok, PEER_SLOTS, ntok), I32),
                   jax.ShapeDtypeStruct((t, PEER_SLOTS), I32), jax.ShapeDtypeStruct((t, PEER_SLOTS), F32)],
        scratch_shapes=[pltpu.VMEM((2 * TOPK_HEAD_UNROLL, PEER_TOPK, ntok), F32),
                        pltpu.VMEM((2 * TOPK_HEAD_UNROLL, PEER_TOPK, ntok), F32),
                        pltpu.VMEM((TOPK_HEAD_UNROLL, PEER_TOPK, ntok), F32),
                        pltpu.VMEM((TOPK_HEAD_UNROLL, PEER_TOPK, ntok), F32),
                        pltpu.VMEM((PEER_SLOTS, ntok), F32), pltpu.VMEM((PEER_SLOTS, ntok), F32),
                        pltpu.VMEM((PEER_SLOTS, ntok), F32)],
        compiler_params=_params("parallel"),
        name="peer_topk",
    )(qp, sk)


TOKEN_GROUP = SUBLANES
BLOCKS_PER_STEP = 2


def _offsets_copy(off_hbm, blk, sm_ref, sem):
    return pltpu.make_async_copy(off_hbm.at[blk], sm_ref, sem)


def _for_each_offset_block(off_hbm, sm_refs, sems, body):
    step, nsteps = pl.program_id(0), pl.num_programs(0)
    first = step * BLOCKS_PER_STEP

    @pl.when(step == 0)
    def _():
        _offsets_copy(off_hbm, 0, sm_refs[0], sems.at[0]).start()

    for b in range(BLOCKS_PER_STEP):
        _offsets_copy(off_hbm, first + b, sm_refs[b], sems.at[b]).wait()
        if b + 1 < BLOCKS_PER_STEP:
            _offsets_copy(off_hbm, first + b + 1, sm_refs[b + 1], sems.at[b + 1]).start()
        else:
            @pl.when(step + 1 < nsteps)
            def _():
                _offsets_copy(off_hbm, first + BLOCKS_PER_STEP, sm_refs[0], sems.at[0]).start()
        body(b, sm_refs[b])


def _expert_rows(tab_ref, sm_ref, shm_ref, tok, j, slot):
    o = pl.multiple_of(sm_ref.at[slot][tok], SUBLANES)
    w = tab_ref[pl.ds(o, SUBLANES), :]
    w = jnp.left_shift(w, shm_ref[j, slot:slot + 1, :]) & HI_MASK
    return lax.bitcast_convert_type(w, F32)


def _lane_broadcast_rows(rows, j):
    return jnp.transpose(jnp.broadcast_to(rows[j:j + 1, :], (LANES, rows.shape[1])))


def _peer_u_kernel(off_hbm, x_ref, sh_ref, pick_ref, fold_ref, tab_ref, act_ref,
                   sm_a, sm_b, sems, shm_a, shm_b, prod_a, prod_b):
    pairs = PEER_BLOCK // (2 * TOKEN_GROUP)

    @pl.when(pl.program_id(0) == 0)
    def _():
        prod_b[...] = jnp.zeros_like(prod_b)

    def block(b, sm_ref):
        base = b * PEER_BLOCK

        def fill(g, shm_ref, prod_ref):
            t0 = pl.multiple_of(g * TOKEN_GROUP, TOKEN_GROUP)
            sh_rows = sh_ref[pl.ds(base + t0, TOKEN_GROUP), :]
            for j in range(TOKEN_GROUP):
                tok = t0 + j
                x = x_ref[pl.ds(pl.multiple_of((base + tok) * SUBLANES, SUBLANES), SUBLANES), :]
                shm_ref[j] = _lane_broadcast_rows(sh_rows, j)
                for slot in range(0, PEER_SLOTS, 2):
                    two = jnp.concatenate([_expert_rows(tab_ref, sm_ref, shm_ref, tok, j, slot) * x,
                                           _expert_rows(tab_ref, sm_ref, shm_ref, tok, j, slot + 1) * x], axis=0)
                    prod_ref[slot * SUBLANES:(slot + 2) * SUBLANES, j * LANES:(j + 1) * LANES] = two.astype(BF16)

        def reduce(g, prod_ref):
            lane_sums = lax.dot_general(pick_ref[...], prod_ref[...], NT_DIMS,
                                        preferred_element_type=F32)
            hi = lane_sums.astype(BF16).astype(F32)
            both = jnp.concatenate([hi, lane_sums - hi], axis=0).astype(BF16)
            res = _dot(both, fold_ref[...])
            rows = pl.ds(pl.multiple_of(base + g * TOKEN_GROUP, TOKEN_GROUP), TOKEN_GROUP)
            act_ref[rows, :] = res[0:TOKEN_GROUP] + res[TOKEN_GROUP:2 * TOKEN_GROUP]

        def pair(i, carry):
            reduce(jnp.maximum(2 * i - 1, 0), prod_b)
            fill(2 * i, shm_a, prod_a)
            reduce(2 * i, prod_a)
            fill(2 * i + 1, shm_b, prod_b)
            return carry

        lax.fori_loop(0, pairs, pair, 0)
        reduce(2 * pairs - 1, prod_b)

    _for_each_offset_block(off_hbm, (sm_a, sm_b), sems, block)


def _gather_scratch():
    sm = pltpu.SMEM((PEER_SLOTS, PEER_BLOCK), I32)
    return [sm, sm, pltpu.SemaphoreType.DMA((BLOCKS_PER_STEP,))]


def _peer_u(off, x8, sh, tab):
    t = sh.shape[0]
    ntok = BLOCKS_PER_STEP * PEER_BLOCK
    width, rows = TOKEN_GROUP * LANES, PEER_SLOTS * SUBLANES
    pick = (jnp.arange(width)[None, :] // LANES == jnp.arange(TOKEN_GROUP)[:, None]).astype(BF16)
    fold = (jnp.arange(rows)[:, None] // SUBLANES == jnp.arange(PEER_SLOTS)[None, :]).astype(BF16)
    slots = pl.BlockSpec((ntok, PEER_SLOTS), lambda i: (i, 0))
    shm = pltpu.VMEM((TOKEN_GROUP, PEER_SLOTS, LANES), I32)
    prod = pltpu.VMEM((rows, width), BF16)
    return pl.pallas_call(
        _peer_u_kernel,
        grid=(t // ntok,),
        in_specs=[pl.BlockSpec(memory_space=pl.ANY),
                  pl.BlockSpec((ntok * SUBLANES, LANES), lambda i: (i, 0)),
                  slots, _resident(pick.shape), _resident(fold.shape), _resident(tab.shape)],
        out_specs=slots,
        out_shape=jax.ShapeDtypeStruct((t, PEER_SLOTS), F32),
        scratch_shapes=_gather_scratch() + [shm, shm, prod, prod],
        compiler_params=_params("arbitrary"),
        name="peer_u",
    )(off, x8, sh, pick, fold, tab)


def _peer_v_kernel(off_hbm, act_ref, gate_ref, sh_ref, h_ref, tab_ref, out_ref,
                   sm_a, sm_b, sems, shm_ref, hm_ref, hid_ref):
    act = act_ref[...]
    gelu = 0.5 * act * (1.0 + lax.erf(act * (2.0 ** -0.5)))
    hid_ref[...] = gate_ref[...] * gelu
    n_acc = 4

    def block(b, sm_ref):
        base = b * PEER_BLOCK

        def group(g, carry):
            t0 = pl.multiple_of(g * TOKEN_GROUP, TOKEN_GROUP)
            sh_rows = sh_ref[pl.ds(base + t0, TOKEN_GROUP), :]
            hid_rows = hid_ref[pl.ds(base + t0, TOKEN_GROUP), :]
            for j in range(TOKEN_GROUP):
                tok = t0 + j
                shm_ref[j] = _lane_broadcast_rows(sh_rows, j)
                hm_ref[j] = _lane_broadcast_rows(hid_rows, j)
                accs = [None] * n_acc
                for slot in range(PEER_SLOTS):
                    term = _expert_rows(tab_ref, sm_ref, shm_ref, tok, j, slot) * hm_ref[j, slot:slot + 1, :]
                    accs[slot % n_acc] = term if accs[slot % n_acc] is None else accs[slot % n_acc] + term
                rows = pl.ds(pl.multiple_of((base + tok) * SUBLANES, SUBLANES), SUBLANES)
                out_ref[rows, :] = h_ref[rows, :] + ((accs[0] + accs[1]) + (accs[2] + accs[3]))
            return carry

        lax.fori_loop(0, PEER_BLOCK // TOKEN_GROUP, group, 0)

    _for_each_offset_block(off_hbm, (sm_a, sm_b), sems, block)


def _peer_v(off, act, gate, sh, h8, tab):
    t = sh.shape[0]
    ntok = BLOCKS_PER_STEP * PEER_BLOCK
    slots = pl.BlockSpec((ntok, PEER_SLOTS), lambda i: (i, 0))
    rows8 = pl.BlockSpec((ntok * SUBLANES, LANES), lambda i: (i, 0))
    return pl.pallas_call(
        _peer_v_kernel,
        grid=(t // ntok,),
        in_specs=[pl.BlockSpec(memory_space=pl.ANY), slots, slots, slots, rows8, _resident(tab.shape)],
        out_specs=rows8,
        out_shape=jax.ShapeDtypeStruct(h8.shape, F32),
        scratch_shapes=_gather_scratch() + [pltpu.VMEM((TOKEN_GROUP, PEER_SLOTS, LANES), I32),
                                            pltpu.VMEM((TOKEN_GROUP, PEER_SLOTS, LANES), F32),
                                            pltpu.VMEM((ntok, PEER_SLOTS), F32)],
        compiler_params=_params("arbitrary"),
        name="peer_v",
    )(off, act, gate, sh, h8, tab)


def _ple_kernel(h_ref, p_ref, g_ref, wg_ref, wp_ref, fin_ref, out_ref, *, final):
    h = jnp.concatenate([h_ref[:, c, :] for c in range(h_ref.shape[1])], axis=1)
    gate = _sigmoid(_dot(_rms(h, g_ref[...]).astype(BF16), wg_ref[...]))
    h2 = h + _dot(p_ref[...].astype(BF16), wp_ref[...]) * gate
    if final:
        h2 = _rms(h2, fin_ref[...])
    out_ref[...] = h2


def _ple(h3, p, g, wg, wp, fin, tm, final):
    t, d = h3.shape[0], h3.shape[1] * h3.shape[2]
    row = lambda w: pl.BlockSpec((tm, w), lambda i: (i, 0))
    return pl.pallas_call(
        functools.partial(_ple_kernel, final=final),
        grid=(t // tm,),
        in_specs=[_tiled_rows(tm, d), row(p.shape[1]), _resident(g.shape), _resident(wg.shape), _resident(wp.shape),
                  _resident(fin.shape)],
        out_specs=row(d),
        out_shape=jax.ShapeDtypeStruct((t, d), F32),
        compiler_params=_params("parallel"),
        name="ple",
    )(h3, p, g, wg, wp, fin)


def _pack_table(tab):
    e, d = tab.shape
    bits = lax.bitcast_convert_type(tab.astype(BF16), jnp.uint16).astype(jnp.uint32)
    word = (bits[: e // 2] << 16) | bits[e // 2:]
    return lax.bitcast_convert_type(word, I32).reshape(e // 2 * (d // LANES), LANES)


def _rel_bias(rel_table):
    dist = jnp.arange(CHUNK - 1 + BAND_CHUNKS * CHUNK, -CHUNK, -1)
    line = rel_table.astype(F32)[:, jnp.clip(dist, -REL_CLIP, REL_CLIP) + REL_CLIP]
    bias = jnp.stack([line[:, CHUNK - 1 - i:CHUNK - 1 - i + BAND] for i in range(CHUNK)], axis=1)
    return bias.reshape(ATT_HEADS // HEADS_PER_TILE, HEADS_PER_TILE * CHUNK, BAND)


def kernel(x, p, norm_mix, w_in, gla_gate_w2, gla_gate_b, gla_head_norm, rel_bias, w_out, norm_ffn, peer_wq,
           peer_subkeys, peer_u, peer_v, norm_ple, w_ple_gate, w_ple, final_norm):
    b, s, d = x.shape
    depth = w_in.shape[0]
    t = b * s
    qk_w, v_w, a_w = GLA_HEADS * GLA_DK, GLA_HEADS * GLA_DV, ATT_HEADS * ATT_DH
    tm = 512
    h = x.reshape(t, d)
    fin = final_norm.reshape(1, d)
    for i in range(depth):
        w = w_in[i]
        c0 = 2 * qk_w + v_w
        wg = jnp.concatenate([w[:, :c0], w[:, c0 + GLA_RANK:c0 + GLA_RANK + v_w]], axis=1).astype(BF16)
        wlr = jnp.pad(w[:, c0:c0 + GLA_RANK], ((0, 0), (0, LANES - GLA_RANK))).astype(BF16)
        wa = w[:, c0 + GLA_RANK + v_w:].astype(BF16)
        gw2 = jnp.pad(gla_gate_w2[i], ((0, LANES - GLA_RANK), (0, 0))).astype(BF16)
        q, k, lg, v, og, aq, ak, av = _in_proj(h, norm_mix[i].reshape(1, d), wg, wlr, gw2,
                                               gla_gate_b[i].reshape(1, qk_w), wa, tm)
        to3 = lambda z: z.reshape(b, s, z.shape[-1])
        o_gla = _gla(to3(q), to3(k), to3(lg), to3(v), to3(og), gla_head_norm[i].reshape(1, GLA_DV))
        front = ((0, 0), (BAND_CHUNKS * CHUNK, 0), (0, 0))
        o_att = _band(to3(aq), jnp.pad(to3(ak), front), jnp.pad(to3(av), front), _rel_bias(rel_bias[i]))
        h, xn, qp = _mix_out(o_gla.reshape(t, v_w), o_att.reshape(t, a_w), h, w_out[i].astype(BF16),
                             norm_ffn[i].reshape(1, d), peer_wq[i].astype(BF16), tm)
        sk = peer_subkeys[i].reshape(PEER_HEADS * 2, N_KEYS, -1).astype(BF16)
        off, sh, gate = _peer_topk(qp, sk)
        rows8 = (t * (d // LANES), LANES)
        act = _peer_u(off, xn.reshape(rows8), sh, _pack_table(peer_u[i]))
        h = _peer_v(off, act, gate, sh, h.reshape(rows8), _pack_table(peer_v[i]))
        h = _ple(h.reshape(t, d // LANES, LANES), p[i].reshape(t, -1), norm_ple[i].reshape(1, d), w_ple_gate[i].astype(BF16),
                 w_ple[i].astype(BF16), fin, tm, final=(i == depth - 1))
    return h.reshape(b, s, d)
```

```python
import functools

import jax
import jax.numpy as jnp
from jax import lax
from jax.experimental import pallas as pl
from jax.experimental.pallas import tpu as pltpu

F32 = jnp.float32
BF16 = jnp.bfloat16
I32 = jnp.int32

RMS_EPS = 1e-6
CHUNK = 64
GLA_HEADS = 4
GLA_DK = 64
GLA_DV = 128
GLA_RANK = 16
GLA_GATE_NORM = 16.0
ATT_HEADS = 8
ATT_DH = 64
BAND_CHUNKS = 8
BAND = (BAND_CHUNKS + 1) * CHUNK
REL_CLIP = 128
PEER_HEADS = 8
N_KEYS = 128
PEER_TOPK = 16
PEER_SLOTS = PEER_HEADS * PEER_TOPK
PEER_BLOCK = 128

LANES = 128
SUBLANES = 8
VMEM_LIMIT_BYTES = 56 * 1024 * 1024

HALF_EXPERTS = N_KEYS * N_KEYS // 2
HI_MASK = -65536

NT_DIMS = (((1,), (1,)), ((), ()))


def _params(*sem):
    return pltpu.CompilerParams(dimension_semantics=sem, vmem_limit_bytes=VMEM_LIMIT_BYTES)


def _resident(shape):
    nd = len(shape)
    return pl.BlockSpec(shape, lambda *_: (0,) * nd, pipeline_mode=pl.Buffered(1))


def _rms(x, g):
    ms = jnp.mean(x * x, axis=-1, keepdims=True)
    return x * lax.rsqrt(ms + RMS_EPS) * g


def _sigmoid(x):
    return 1.0 / (1.0 + jnp.exp(-x))


def _dot(a, b):
    return jnp.dot(a, b, preferred_element_type=F32)


def _in_proj_kernel(x_ref, g_ref, wg_ref, wlr_ref, gw2_ref, gb_ref, wa_ref,
                    q_ref, k_ref, lg_ref, v_ref, og_ref, aq_ref, ak_ref, av_ref):
    hn = _rms(x_ref[...], g_ref[...]).astype(BF16)
    gla = _dot(hn, wg_ref[...])
    qk_w = GLA_HEADS * GLA_DK
    v_w = GLA_HEADS * GLA_DV
    q_ref[...] = gla[:, 0:qk_w]
    k_ref[...] = gla[:, qk_w:2 * qk_w]
    v_ref[...] = gla[:, 2 * qk_w:2 * qk_w + v_w]
    og_ref[...] = gla[:, 2 * qk_w + v_w:2 * qk_w + 2 * v_w]
    lr = _dot(hn, wlr_ref[...])
    graw = _dot(lr.astype(BF16), gw2_ref[...]) + gb_ref[...]
    lg_ref[...] = (jnp.minimum(graw, 0.0) - jnp.log(1.0 + jnp.exp(-jnp.abs(graw)))) * (1.0 / GLA_GATE_NORM)
    att = _dot(hn, wa_ref[...])
    a_w = ATT_HEADS * ATT_DH
    aq_ref[...] = att[:, 0:a_w].astype(BF16)
    ak_ref[...] = att[:, a_w:2 * a_w].astype(BF16)
    av_ref[...] = att[:, 2 * a_w:3 * a_w].astype(BF16)


def _in_proj(x, g, wg, wlr, gw2, gb, wa, tm):
    t, d = x.shape
    qk_w, v_w, a_w = GLA_HEADS * GLA_DK, GLA_HEADS * GLA_DV, ATT_HEADS * ATT_DH
    row = lambda w: pl.BlockSpec((tm, w), lambda i: (i, 0))
    outs = [(qk_w, F32), (qk_w, F32), (qk_w, F32), (v_w, F32), (v_w, F32), (a_w, BF16), (a_w, BF16), (a_w, BF16)]
    return pl.pallas_call(
        _in_proj_kernel,
        grid=(t // tm,),
        in_specs=[row(d), _resident(g.shape), _resident(wg.shape), _resident(wlr.shape),
                  _resident(gw2.shape), _resident(gb.shape), _resident(wa.shape)],
        out_specs=[row(w) for w, _ in outs],
        out_shape=[jax.ShapeDtypeStruct((t, w), dt) for w, dt in outs],
        compiler_params=_params("parallel"),
        name="in_proj",
    )(x, g, wg, wlr, gw2, gb, wa)


GLA_ROWS_PER_STEP = 4


def _gla_kernel(q_ref, k_ref, lg_ref, v_ref, og_ref, hn_ref, o_ref, s_ref):
    c = pl.program_id(1)

    @pl.when(c == 0)
    def _():
        s_ref[...] = jnp.zeros_like(s_ref)

    for r in range(GLA_ROWS_PER_STEP):
        _gla_chunk(q_ref.at[r], k_ref.at[r], lg_ref.at[r], v_ref.at[r], og_ref.at[r], hn_ref, o_ref.at[r],
                   s_ref.at[r])


def _gla_chunk(q_ref, k_ref, lg_ref, v_ref, og_ref, hn_ref, o_ref, s_ref):
    qk_w, v_w = GLA_HEADS * GLA_DK, GLA_HEADS * GLA_DV
    q, k, g = q_ref[...], k_ref[...], lg_ref[...]
    r_i = lax.broadcasted_iota(I32, (CHUNK, CHUNK), 0)
    c_i = lax.broadcasted_iota(I32, (CHUNK, CHUNK), 1)
    tri = (c_i <= r_i).astype(BF16)
    g_hi = g.astype(BF16)
    g_lo = (g - g_hi.astype(F32)).astype(BF16)
    cum = _dot(tri, g_hi) + _dot(tri, g_lo)
    mid = cum[CHUNK // 2:CHUNK // 2 + 1, :]
    last = cum[CHUNK - 1:CHUNK, :]
    scale = GLA_DK ** -0.5
    qe = q * jnp.exp(cum - mid) * scale
    ke = (k * jnp.exp(mid - cum)).astype(BF16)
    kst = k * jnp.exp(last - cum)
    qc = (q * jnp.exp(cum) * scale).astype(BF16)
    v = v_ref[...].astype(BF16)

    lane_head = lax.broadcasted_iota(I32, (CHUNK, qk_w), 1) // GLA_DK
    q_stack = jnp.concatenate(
        [jnp.where(lane_head == h, qe, 0.0) for h in range(GLA_HEADS)], axis=0).astype(BF16)
    a = lax.dot_general(q_stack, ke, NT_DIMS, preferred_element_type=F32)
    row_in_chunk = lax.broadcasted_iota(I32, (GLA_HEADS * CHUNK, CHUNK), 0) % CHUNK
    col = lax.broadcasted_iota(I32, (GLA_HEADS * CHUNK, CHUNK), 1)
    a = jnp.where(col <= row_in_chunk, a, 0.0).astype(BF16)
    p = _dot(a, v)
    o_intra = jnp.concatenate(
        [p[h * CHUNK:(h + 1) * CHUNK, h * GLA_DV:(h + 1) * GLA_DV] for h in range(GLA_HEADS)], axis=1)

    s = s_ref[...]
    o_inter = _dot(qc, s.astype(BF16))
    kv = _dot(jnp.transpose(kst).astype(BF16), v)
    s_row_head = lax.broadcasted_iota(I32, (qk_w, v_w), 0) // GLA_DK
    s_col_head = lax.broadcasted_iota(I32, (qk_w, v_w), 1) // GLA_DV
    dec = jnp.exp(last)
    dec_col = jnp.transpose(jnp.broadcast_to(dec, (LANES, qk_w)))
    dec_full = jnp.concatenate([dec_col] * (v_w // LANES), axis=1)
    s_ref[...] = s * dec_full + jnp.where(s_row_head == s_col_head, kv, 0.0)

    o = o_intra + o_inter
    og = og_ref[...]
    silu = og * _sigmoid(og)
    outs = []
    for h in range(GLA_HEADS):
        oh = o[:, h * GLA_DV:(h + 1) * GLA_DV]
        outs.append(_rms(oh, hn_ref[...]))
    o_ref[...] = (jnp.concatenate(outs, axis=1) * silu).astype(o_ref.dtype)


def _gla(q, k, lg, v, og, hn):
    b, s, qk_w = q.shape
    v_w = v.shape[-1]
    blk = lambda w: pl.BlockSpec((GLA_ROWS_PER_STEP, CHUNK, w), lambda i, j: (i, j, 0))
    return pl.pallas_call(
        _gla_kernel,
        grid=(b // GLA_ROWS_PER_STEP, s // CHUNK),
        in_specs=[blk(qk_w), blk(qk_w), blk(qk_w), blk(v_w), blk(v_w), _resident(hn.shape)],
        out_specs=blk(v_w),
        out_shape=jax.ShapeDtypeStruct((b, s, v_w), BF16),
        scratch_shapes=[pltpu.VMEM((GLA_ROWS_PER_STEP, qk_w, v_w), F32)],
        compiler_params=_params("parallel", "arbitrary"),
        name="gla",
    )(q, k, lg, v, og, hn)


BAND_CHUNKS_PER_STEP = 2
HEADS_PER_TILE = LANES // ATT_DH


def _band_kernel(q_ref, k_ref, v_ref, bias_ref, o_ref):
    step = pl.program_id(1)
    pad = BAND_CHUNKS * CHUNK
    rows = HEADS_PER_TILE * CHUNK
    low = lax.broadcasted_iota(I32, (CHUNK, LANES), 1) < ATT_DH
    scale = ATT_DH ** -0.5
    for u in range(BAND_CHUNKS_PER_STEP):
        c = step * BAND_CHUNKS_PER_STEP + u
        start = pl.multiple_of(c * CHUNK, CHUNK)
        q = q_ref[u * CHUNK:(u + 1) * CHUNK, :]
        kw = k_ref[pl.ds(start, BAND), :]
        vw = v_ref[pl.ds(start, BAND), :]
        valid = lax.broadcasted_iota(I32, (rows, BAND), 1) >= (pad - c * CHUNK)
        outs = []
        for tile in range(ATT_HEADS // HEADS_PER_TILE):
            sl = slice(tile * LANES, (tile + 1) * LANES)
            q2, k2, v2 = q[:, sl], kw[:, sl], vw[:, sl]
            zero = jnp.zeros_like(q2)
            q_stack = jnp.concatenate([jnp.where(low, q2, zero), jnp.where(low, zero, q2)], axis=0)
            s = lax.dot_general(q_stack, k2, NT_DIMS, preferred_element_type=F32)
            s = jnp.where(valid, s * scale + bias_ref[tile], -1e30)
            e = jnp.exp(s - jnp.max(s, axis=-1, keepdims=True))
            den = jnp.sum(e, axis=-1, keepdims=True)
            pv = _dot(e.astype(BF16), v2) / den
            outs.append(jnp.where(low, pv[0:CHUNK], pv[CHUNK:2 * CHUNK]))
        o_ref[u * CHUNK:(u + 1) * CHUNK, :] = jnp.concatenate(outs, axis=1).astype(o_ref.dtype)


def _band(q, kp, vp, bias):
    b, s, w = q.shape
    sp = kp.shape[1]
    rows = BAND_CHUNKS_PER_STEP * CHUNK
    return pl.pallas_call(
        _band_kernel,
        grid=(b, s // rows),
        in_specs=[pl.BlockSpec((None, rows, w), lambda i, j: (i, j, 0)),
                  pl.BlockSpec((None, sp, w), lambda i, j: (i, 0, 0)),
                  pl.BlockSpec((None, sp, w), lambda i, j: (i, 0, 0)),
                  _resident(bias.shape)],
        out_specs=pl.BlockSpec((None, rows, w), lambda i, j: (i, j, 0)),
        out_shape=jax.ShapeDtypeStruct((b, s, w), BF16),
        compiler_params=_params("parallel", "arbitrary"),
        name="band",
    )(q, kp, vp, bias)


def _mix_out_kernel(og_ref, oa_ref, h_ref, wo_ref, gf_ref, wq_ref, h2_ref, xn_ref, qp_ref):
    v_w = og_ref.shape[-1]
    o = _dot(og_ref[...], wo_ref[0:v_w, :]) + _dot(oa_ref[...], wo_ref[v_w:, :])
    h2 = h_ref[...] + o
    xn = _rms(h2, gf_ref[...])
    for c in range(h2_ref.shape[1]):
        h2_ref[:, c, :] = h2[:, c * LANES:(c + 1) * LANES]
        xn_ref[:, c, :] = xn[:, c * LANES:(c + 1) * LANES]
    qp = _dot(xn.astype(BF16), wq_ref[...])
    for i in range(qp_ref.shape[0]):
        qp_ref[i] = qp[:, i * LANES:(i + 1) * LANES].astype(BF16)


def _tiled_rows(tm, d):
    return pl.BlockSpec((tm, d // LANES, LANES), lambda i: (i, 0, 0))


def _mix_out(og, oa, h, wo, gf, wq, tm):
    t, d = h.shape
    n_sub = wq.shape[1] // LANES
    row = lambda w: pl.BlockSpec((tm, w), lambda i: (i, 0))
    return pl.pallas_call(
        _mix_out_kernel,
        grid=(t // tm,),
        in_specs=[row(og.shape[1]), row(oa.shape[1]), row(d), _resident(wo.shape), _resident(gf.shape),
                  _resident(wq.shape)],
        out_specs=[_tiled_rows(tm, d), _tiled_rows(tm, d), pl.BlockSpec((n_sub, tm, LANES), lambda i: (0, i, 0))],
        out_shape=[jax.ShapeDtypeStruct((t, d // LANES, LANES), F32),
                   jax.ShapeDtypeStruct((t, d // LANES, LANES), F32),
                   jax.ShapeDtypeStruct((n_sub, t, LANES), BF16)],
        compiler_params=_params("parallel"),
        name="mix_out",
    )(og, oa, h, wo, gf, wq)


_GRID_REGS = (
    ("a", 0, 0, 0, 8), ("a", 0, 8, 0, 8), ("a", 1, 0, 0, 8), ("b", 0, 8, 0, 8),
    ("a", 2, 0, 0, 5), ("a", 3, 0, 0, 4), ("b", 0, 0, 4, 8), ("b", 1, 0, 4, 8), ("b", 2, 0, 4, 5),
)


TOPK_HEAD_UNROLL = 8


def _top_rows(s, iota_rows, count):
    n = s.shape[0]
    vals, ids = [], []
    for _ in range(count):
        m = jnp.max(s, axis=0, keepdims=True)
        idx = jnp.min(jnp.where(s == m, iota_rows, float(n)), axis=0, keepdims=True)
        s = jnp.where(iota_rows == idx, -jnp.inf, s)
        vals.append(m)
        ids.append(idx)
    return vals, ids


def _peer_topk_kernel(q_ref, sk_ref, off_ref, sh_ref, gate_ref, s_scr, i_scr, t_scr, e_scr, g_all, o_all, h_all):
    ntok = q_ref.shape[1]
    iota_keys = lax.broadcasted_iota(I32, (N_KEYS, ntok), 0).astype(F32)
    sub = lax.broadcasted_iota(I32, (SUBLANES, ntok), 0)
    subf = sub.astype(F32)

    def one_head(h, u):
        for half in range(2):
            sc = lax.dot_general(sk_ref[2 * h + half], q_ref[2 * h + half], NT_DIMS,
                                 preferred_element_type=F32)
            vals, ids = _top_rows(sc, iota_keys, PEER_TOPK)
            for kk in range(PEER_TOPK):
                s_scr[2 * u + half, kk:kk + 1, :] = vals[kk]
                i_scr[2 * u + half, kk:kk + 1, :] = ids[kk]
        cand, flat, eid = [], [], []
        for axis, fixed, first, lo, hi in _GRID_REGS:
            run = slice(first, first + SUBLANES)
            if axis == "a":
                val = s_scr[2 * u, fixed:fixed + 1, :] + s_scr[2 * u + 1, run, :]
                ee = i_scr[2 * u, fixed:fixed + 1, :] * float(N_KEYS) + i_scr[2 * u + 1, run, :]
                ff = float(fixed * PEER_TOPK + first) + subf
            else:
                val = s_scr[2 * u, run, :] + s_scr[2 * u + 1, fixed:fixed + 1, :]
                ee = i_scr[2 * u, run, :] * float(N_KEYS) + i_scr[2 * u + 1, fixed:fixed + 1, :]
                ff = (float(first) + subf) * float(PEER_TOPK) + float(fixed)
            if (lo, hi) != (0, SUBLANES):
                val = jnp.where((sub >= lo) & (sub < hi), val, -jnp.inf)
            cand.append(val)
            flat.append(ff)
            eid.append(ee)
        big = float(PEER_TOPK * PEER_TOPK)
        for kk in range(PEER_TOPK):
            m = jnp.max(functools.reduce(jnp.maximum, cand), axis=0, keepdims=True)
            fsel = [jnp.where(cv == m, fv, big) for cv, fv in zip(cand, flat)]
            fm = jnp.min(functools.reduce(jnp.minimum, fsel), axis=0, keepdims=True)
            hit = [fv == fm for fv in flat]
            esel = [jnp.where(hv, ev, -1.0) for hv, ev in zip(hit, eid)]
            e = jnp.max(functools.reduce(jnp.maximum, esel), axis=0, keepdims=True)
            cand = [jnp.where(hv, -jnp.inf, cv) for hv, cv in zip(hit, cand)]
            t_scr[u, kk:kk + 1, :] = m
            e_scr[u, kk:kk + 1, :] = e
        top = t_scr[u]
        ex = jnp.exp(top - top[0:1, :])
        gate = ex / jnp.sum(ex, axis=0, keepdims=True)
        e = e_scr[u]
        in_low_half = (e >= float(HALF_EXPERTS)).astype(F32)
        rows = pl.ds(pl.multiple_of(h * PEER_TOPK, PEER_TOPK), PEER_TOPK)
        g_all[rows, :] = gate
        o_all[rows, :] = (e - in_low_half * float(HALF_EXPERTS)) * float(SUBLANES)
        h_all[rows, :] = in_low_half * 16.0

    def head_group(g, carry):
        for u in range(TOPK_HEAD_UNROLL):
            one_head(g * TOPK_HEAD_UNROLL + u, u)
        return carry

    lax.fori_loop(0, PEER_HEADS // TOPK_HEAD_UNROLL, head_group, 0)
    gate_ref[...] = jnp.transpose(g_all[...])
    off_ref[...] = o_all[...].astype(I32)
    sh_ref[...] = jnp.transpose(h_all[...]).astype(I32)


def _peer_topk(qp, sk):
    n_sub, t, _ = qp.shape
    ntok = PEER_BLOCK
    out = pl.BlockSpec((ntok, PEER_SLOTS), lambda i: (i, 0))
    return pl.pallas_call(
        _peer_topk_kernel,
        grid=(t // ntok,),
        in_specs=[pl.BlockSpec((n_sub, ntok, LANES), lambda i: (0, i, 0)), _resident(sk.shape)],
        out_specs=[pl.BlockSpec((None, PEER_SLOTS, ntok), lambda i: (i, 0, 0)), out, out],
        out_shape=[jax.ShapeDtypeStruct((t // ntok, PEER_SLOTS, ntok), I32),
                   jax.ShapeDtypeStruct((t, PEER_SLOTS), I32), jax.ShapeDtypeStruct((t, PEER_SLOTS), F32)],
        scratch_shapes=[pltpu.VMEM((2 * TOPK_HEAD_UNROLL, PEER_TOPK, ntok), F32),
                        pltpu.VMEM((2 * TOPK_HEAD_UNROLL, PEER_TOPK, ntok), F32),
                        pltpu.VMEM((TOPK_HEAD_UNROLL, PEER_TOPK, ntok), F32),
                        pltpu.VMEM((TOPK_HEAD_UNROLL, PEER_TOPK, ntok), F32),
                        pltpu.VMEM((PEER_SLOTS, ntok), F32), pltpu.VMEM((PEER_SLOTS, ntok), F32),
                        pltpu.VMEM((PEER_SLOTS, ntok), F32)],
        compiler_params=_params("parallel"),
        name="peer_topk",
    )(qp, sk)


TOKEN_GROUP = SUBLANES
BLOCKS_PER_STEP = 2


def _offsets_copy(off_hbm, blk, sm_ref, sem):
    return pltpu.make_async_copy(off_hbm.at[blk], sm_ref, sem)


def _for_each_offset_block(off_hbm, sm_refs, sems, body):
    step, nsteps = pl.program_id(0), pl.num_programs(0)
    first = step * BLOCKS_PER_STEP

    @pl.when(step == 0)
    def _():
        _offsets_copy(off_hbm, 0, sm_refs[0], sems.at[0]).start()

    for b in range(BLOCKS_PER_STEP):
        _offsets_copy(off_hbm, first + b, sm_refs[b], sems.at[b]).wait()
        if b + 1 < BLOCKS_PER_STEP:
            _offsets_copy(off_hbm, first + b + 1, sm_refs[b + 1], sems.at[b + 1]).start()
        else:
            @pl.when(step + 1 < nsteps)
            def _():
                _offsets_copy(off_hbm, first + BLOCKS_PER_STEP, sm_refs[0], sems.at[0]).start()
        body(b, sm_refs[b])


def _expert_rows(tab_ref, sm_ref, shm_ref, tok, j, slot):
    o = pl.multiple_of(sm_ref.at[slot][tok], SUBLANES)
    w = tab_ref[pl.ds(o, SUBLANES), :]
    w = jnp.left_shift(w, shm_ref[j, slot:slot + 1, :]) & HI_MASK
    return lax.bitcast_convert_type(w, F32)


def _lane_broadcast_rows(rows, j):
    return jnp.transpose(jnp.broadcast_to(rows[j:j + 1, :], (LANES, rows.shape[1])))


def _peer_u_kernel(off_hbm, x_ref, sh_ref, pick_ref, fold_ref, tab_ref, act_ref,
                   sm_a, sm_b, sems, shm_a, shm_b, prod_a, prod_b):
    pairs = PEER_BLOCK // (2 * TOKEN_GROUP)

    @pl.when(pl.program_id(0) == 0)
    def _():
        prod_b[...] = jnp.zeros_like(prod_b)

    def block(b, sm_ref):
        base = b * PEER_BLOCK

        def fill(g, shm_ref, prod_ref):
            t0 = pl.multiple_of(g * TOKEN_GROUP, TOKEN_GROUP)
            sh_rows = sh_ref[pl.ds(base + t0, TOKEN_GROUP), :]
            for j in range(TOKEN_GROUP):
                tok = t0 + j
                x = x_ref[pl.ds(pl.multiple_of((base + tok) * SUBLANES, SUBLANES), SUBLANES), :]
                shm_ref[j] = _lane_broadcast_rows(sh_rows, j)
                for slot in range(0, PEER_SLOTS, 2):
                    two = jnp.concatenate([_expert_rows(tab_ref, sm_ref, shm_ref, tok, j, slot) * x,
                                           _expert_rows(tab_ref, sm_ref, shm_ref, tok, j, slot + 1) * x], axis=0)
                    prod_ref[slot * SUBLANES:(slot + 2) * SUBLANES, j * LANES:(j + 1) * LANES] = two.astype(BF16)

        def reduce(g, prod_ref):
            lane_sums = lax.dot_general(pick_ref[...], prod_ref[...], NT_DIMS,
                                        preferred_element_type=F32)
            hi = lane_sums.astype(BF16).astype(F32)
            both = jnp.concatenate([hi, lane_sums - hi], axis=0).astype(BF16)
            res = _dot(both, fold_ref[...])
            rows = pl.ds(pl.multiple_of(base + g * TOKEN_GROUP, TOKEN_GROUP), TOKEN_GROUP)
            act_ref[rows, :] = res[0:TOKEN_GROUP] + res[TOKEN_GROUP:2 * TOKEN_GROUP]

        def pair(i, carry):
            reduce(jnp.maximum(2 * i - 1, 0), prod_b)
            fill(2 * i, shm_a, prod_a)
            reduce(2 * i, prod_a)
            fill(2 * i + 1, shm_b, prod_b)
            return carry

        lax.fori_loop(0, pairs, pair, 0)
        reduce(2 * pairs - 1, prod_b)

    _for_each_offset_block(off_hbm, (sm_a, sm_b), sems, block)


def _gather_scratch():
    sm = pltpu.SMEM((PEER_SLOTS, PEER_BLOCK), I32)
    return [sm, sm, pltpu.SemaphoreType.DMA((BLOCKS_PER_STEP,))]


def _peer_u(off, x8, sh, tab):
    t = sh.shape[0]
    ntok = BLOCKS_PER_STEP * PEER_BLOCK
    width, rows = TOKEN_GROUP * LANES, PEER_SLOTS * SUBLANES
    pick = (jnp.arange(width)[None, :] // LANES == jnp.arange(TOKEN_GROUP)[:, None]).astype(BF16)
    fold = (jnp.arange(rows)[:, None] // SUBLANES == jnp.arange(PEER_SLOTS)[None, :]).astype(BF16)
    slots = pl.BlockSpec((ntok, PEER_SLOTS), lambda i: (i, 0))
    shm = pltpu.VMEM((TOKEN_GROUP, PEER_SLOTS, LANES), I32)
    prod = pltpu.VMEM((rows, width), BF16)
    return pl.pallas_call(
        _peer_u_kernel,
        grid=(t // ntok,),
        in_specs=[pl.BlockSpec(memory_space=pl.ANY),
                  pl.BlockSpec((ntok * SUBLANES, LANES), lambda i: (i, 0)),
                  slots, _resident(pick.shape), _resident(fold.shape), _resident(tab.shape)],
        out_specs=slots,
        out_shape=jax.ShapeDtypeStruct((t, PEER_SLOTS), F32),
        scratch_shapes=_gather_scratch() + [shm, shm, prod, prod],
        compiler_params=_params("arbitrary"),
        name="peer_u",
    )(off, x8, sh, pick, fold, tab)


def _peer_v_kernel(off_hbm, act_ref, gate_ref, sh_ref, h_ref, tab_ref, out_ref,
                   sm_a, sm_b, sems, shm_a, shm_b, hm_a, hm_b, hid_ref):
    act = act_ref[...]
    gelu = 0.5 * act * (1.0 + lax.erf(act * (2.0 ** -0.5)))
    hid_ref[...] = gate_ref[...] * gelu
    n_acc = 4
    groups = PEER_BLOCK // TOKEN_GROUP

    def block(b, sm_ref):
        base = b * PEER_BLOCK

        def spread(g, shm_ref, hm_ref):
            t0 = pl.multiple_of(g * TOKEN_GROUP, TOKEN_GROUP)
            sh_rows = sh_ref[pl.ds(base + t0, TOKEN_GROUP), :]
            hid_rows = hid_ref[pl.ds(base + t0, TOKEN_GROUP), :]
            for j in range(TOKEN_GROUP):
                shm_ref[j] = _lane_broadcast_rows(sh_rows, j)
                hm_ref[j] = _lane_broadcast_rows(hid_rows, j)

        def gather(g, shm_ref, hm_ref):
            t0 = pl.multiple_of(g * TOKEN_GROUP, TOKEN_GROUP)
            for j in range(TOKEN_GROUP):
                tok = t0 + j
                accs = [None] * n_acc
                for slot in range(PEER_SLOTS):
                    term = _expert_rows(tab_ref, sm_ref, shm_ref, tok, j, slot) * hm_ref[j, slot:slot + 1, :]
                    accs[slot % n_acc] = term if accs[slot % n_acc] is None else accs[slot % n_acc] + term
                rows = pl.ds(pl.multiple_of((base + tok) * SUBLANES, SUBLANES), SUBLANES)
                out_ref[rows, :] = h_ref[rows, :] + ((accs[0] + accs[1]) + (accs[2] + accs[3]))

        def pair(i, carry):
            spread(2 * i + 1, shm_b, hm_b)
            gather(2 * i, shm_a, hm_a)
            spread(jnp.minimum(2 * i + 2, groups - 1), shm_a, hm_a)
            gather(2 * i + 1, shm_b, hm_b)
            return carry

        spread(0, shm_a, hm_a)
        lax.fori_loop(0, groups // 2, pair, 0)

    _for_each_offset_block(off_hbm, (sm_a, sm_b), sems, block)


def _peer_v(off, act, gate, sh, h8, tab):
    t = sh.shape[0]
    ntok = BLOCKS_PER_STEP * PEER_BLOCK
    slots = pl.BlockSpec((ntok, PEER_SLOTS), lambda i: (i, 0))
    rows8 = pl.BlockSpec((ntok * SUBLANES, LANES), lambda i: (i, 0))
    return pl.pallas_call(
        _peer_v_kernel,
        grid=(t // ntok,),
        in_specs=[pl.BlockSpec(memory_space=pl.ANY), slots, slots, slots, rows8, _resident(tab.shape)],
        out_specs=rows8,
        out_shape=jax.ShapeDtypeStruct(h8.shape, F32),
        scratch_shapes=_gather_scratch() + [pltpu.VMEM((TOKEN_GROUP, PEER_SLOTS, LANES), I32),
                                            pltpu.VMEM((TOKEN_GROUP, PEER_SLOTS, LANES), I32),
                                            pltpu.VMEM((TOKEN_GROUP, PEER_SLOTS, LANES), F32),
                                            pltpu.VMEM((TOKEN_GROUP, PEER_SLOTS, LANES), F32),
                                            pltpu.VMEM((ntok, PEER_SLOTS), F32)],
        compiler_params=_params("arbitrary"),
        name="peer_v",
    )(off, act, gate, sh, h8, tab)


def _ple_kernel(h_ref, p_ref, g_ref, wg_ref, wp_ref, fin_ref, out_ref, *, final):
    h = jnp.concatenate([h_ref[:, c, :] for c in range(h_ref.shape[1])], axis=1)
    gate = _sigmoid(_dot(_rms(h, g_ref[...]).astype(BF16), wg_ref[...]))
    h2 = h + _dot(p_ref[...].astype(BF16), wp_ref[...]) * gate
    if final:
        h2 = _rms(h2, fin_ref[...])
    out_ref[...] = h2


def _ple(h3, p, layer, g, wg, wp, fin, tm, final):
    t, d = h3.shape[0], h3.shape[1] * h3.shape[2]
    row = lambda w: pl.BlockSpec((tm, w), lambda i: (i, 0))
    return pl.pallas_call(
        functools.partial(_ple_kernel, final=final),
        grid=(t // tm,),
        in_specs=[_tiled_rows(tm, d), pl.BlockSpec((None, tm, p.shape[2]), lambda i: (layer, i, 0)),
                  _resident(g.shape), _resident(wg.shape), _resident(wp.shape),
                  _resident(fin.shape)],
        out_specs=row(d),
        out_shape=jax.ShapeDtypeStruct((t, d), F32),
        compiler_params=_params("parallel"),
        name="ple",
    )(h3, p, g, wg, wp, fin)


def _pack_table(tab):
    e, d = tab.shape
    bits = lax.bitcast_convert_type(tab.astype(BF16), jnp.uint16).astype(jnp.uint32)
    word = (bits[: e // 2] << 16) | bits[e // 2:]
    return lax.bitcast_convert_type(word, I32).reshape(e // 2 * (d // LANES), LANES)


def _rel_bias(rel_table):
    dist = jnp.arange(CHUNK - 1 + BAND_CHUNKS * CHUNK, -CHUNK, -1)
    line = rel_table.astype(F32)[:, jnp.clip(dist, -REL_CLIP, REL_CLIP) + REL_CLIP]
    bias = jnp.stack([line[:, CHUNK - 1 - i:CHUNK - 1 - i + BAND] for i in range(CHUNK)], axis=1)
    return bias.reshape(ATT_HEADS // HEADS_PER_TILE, HEADS_PER_TILE * CHUNK, BAND)


def kernel(x, p, norm_mix, w_in, gla_gate_w2, gla_gate_b, gla_head_norm, rel_bias, w_out, norm_ffn, peer_wq,
           peer_subkeys, peer_u, peer_v, norm_ple, w_ple_gate, w_ple, final_norm):
    b, s, d = x.shape
    depth = w_in.shape[0]
    t = b * s
    qk_w, v_w, a_w = GLA_HEADS * GLA_DK, GLA_HEADS * GLA_DV, ATT_HEADS * ATT_DH
    tm = 512
    h = x.reshape(t, d)
    fin = final_norm.reshape(1, d)
    for i in range(depth):
        w = w_in[i]
        c0 = 2 * qk_w + v_w
        wg = jnp.concatenate([w[:, :c0], w[:, c0 + GLA_RANK:c0 + GLA_RANK + v_w]], axis=1).astype(BF16)
        wlr = jnp.pad(w[:, c0:c0 + GLA_RANK], ((0, 0), (0, LANES - GLA_RANK))).astype(BF16)
        wa = w[:, c0 + GLA_RANK + v_w:].astype(BF16)
        gw2 = jnp.pad(gla_gate_w2[i], ((0, LANES - GLA_RANK), (0, 0))).astype(BF16)
        q, k, lg, v, og, aq, ak, av = _in_proj(h, norm_mix[i].reshape(1, d), wg, wlr, gw2,
                                               gla_gate_b[i].reshape(1, qk_w), wa, tm)
        to3 = lambda z: z.reshape(b, s, z.shape[-1])
        o_gla = _gla(to3(q), to3(k), to3(lg), to3(v), to3(og), gla_head_norm[i].reshape(1, GLA_DV))
        front = ((0, 0), (BAND_CHUNKS * CHUNK, 0), (0, 0))
        o_att = _band(to3(aq), jnp.pad(to3(ak), front), jnp.pad(to3(av), front), _rel_bias(rel_bias[i]))
        h, xn, qp = _mix_out(o_gla.reshape(t, v_w), o_att.reshape(t, a_w), h, w_out[i].astype(BF16),
                             norm_ffn[i].reshape(1, d), peer_wq[i].astype(BF16), tm)
        sk = peer_subkeys[i].reshape(PEER_HEADS * 2, N_KEYS, -1).astype(BF16)
        off, sh, gate = _peer_topk(qp, sk)
        rows8 = (t * (d // LANES), LANES)
        act = _peer_u(off, xn.reshape(rows8), sh, _pack_table(peer_u[i]))
        h = _peer_v(off, act, gate, sh, h.reshape(rows8), _pack_table(peer_v[i]))
        h = _ple(h.reshape(t, d // LANES, LANES), p.reshape(depth, t, -1), i, norm_ple[i].reshape(1, d),
                 w_ple_gate[i].astype(BF16), w_ple[i].astype(BF16), fin, tm, final=(i == depth - 1))
    return h.reshape(b, s, d)
```

```python
import functools

import jax
import jax.numpy as jnp
from jax import lax
from jax.experimental import pallas as pl
from jax.experimental.pallas import tpu as pltpu

F32 = jnp.float32
BF16 = jnp.bfloat16
I32 = jnp.int32

RMS_EPS = 1e-6
CHUNK = 64
GLA_HEADS = 4
GLA_DK = 64
GLA_DV = 128
GLA_RANK = 16
GLA_GATE_NORM = 16.0
ATT_HEADS = 8
ATT_DH = 64
BAND_CHUNKS = 8
BAND = (BAND_CHUNKS + 1) * CHUNK
REL_CLIP = 128
PEER_HEADS = 8
N_KEYS = 128
PEER_TOPK = 16
PEER_SLOTS = PEER_HEADS * PEER_TOPK
PEER_BLOCK = 128

LANES = 128
SUBLANES = 8
VMEM_LIMIT_BYTES = 56 * 1024 * 1024

HALF_EXPERTS = N_KEYS * N_KEYS // 2
HI_MASK = -65536

NT_DIMS = (((1,), (1,)), ((), ()))


def _params(*sem):
    return pltpu.CompilerParams(dimension_semantics=sem, vmem_limit_bytes=VMEM_LIMIT_BYTES)


def _resident(shape):
    nd = len(shape)
    return pl.BlockSpec(shape, lambda *_: (0,) * nd, pipeline_mode=pl.Buffered(1))


def _rms(x, g):
    ms = jnp.mean(x * x, axis=-1, keepdims=True)
    return x * lax.rsqrt(ms + RMS_EPS) * g


def _sigmoid(x):
    return 1.0 / (1.0 + jnp.exp(-x))


def _dot(a, b):
    return jnp.dot(a, b, preferred_element_type=F32)


def _in_proj_kernel(x_ref, g_ref, wg_ref, wlr_ref, gw2_ref, gb_ref, wa_ref,
                    q_ref, k_ref, lg_ref, v_ref, og_ref, aq_ref, ak_ref, av_ref):
    hn = _rms(x_ref[...], g_ref[...]).astype(BF16)
    gla = _dot(hn, wg_ref[...])
    qk_w = GLA_HEADS * GLA_DK
    v_w = GLA_HEADS * GLA_DV
    q_ref[...] = gla[:, 0:qk_w]
    k_ref[...] = gla[:, qk_w:2 * qk_w]
    v_ref[...] = gla[:, 2 * qk_w:2 * qk_w + v_w]
    og_ref[...] = gla[:, 2 * qk_w + v_w:2 * qk_w + 2 * v_w]
    lr = _dot(hn, wlr_ref[...])
    graw = _dot(lr.astype(BF16), gw2_ref[...]) + gb_ref[...]
    lg_ref[...] = (jnp.minimum(graw, 0.0) - jnp.log(1.0 + jnp.exp(-jnp.abs(graw)))) * (1.0 / GLA_GATE_NORM)
    att = _dot(hn, wa_ref[...])
    a_w = ATT_HEADS * ATT_DH
    aq_ref[...] = att[:, 0:a_w].astype(BF16)
    ak_ref[...] = att[:, a_w:2 * a_w].astype(BF16)
    av_ref[...] = att[:, 2 * a_w:3 * a_w].astype(BF16)


def _in_proj(x, g, wg, wlr, gw2, gb, wa, tm):
    t, d = x.shape
    qk_w, v_w, a_w = GLA_HEADS * GLA_DK, GLA_HEADS * GLA_DV, ATT_HEADS * ATT_DH
    row = lambda w: pl.BlockSpec((tm, w), lambda i: (i, 0))
    outs = [(qk_w, F32), (qk_w, F32), (qk_w, F32), (v_w, F32), (v_w, F32), (a_w, BF16), (a_w, BF16), (a_w, BF16)]
    return pl.pallas_call(
        _in_proj_kernel,
        grid=(t // tm,),
        in_specs=[row(d), _resident(g.shape), _resident(wg.shape), _resident(wlr.shape),
                  _resident(gw2.shape), _resident(gb.shape), _resident(wa.shape)],
        out_specs=[row(w) for w, _ in outs],
        out_shape=[jax.ShapeDtypeStruct((t, w), dt) for w, dt in outs],
        compiler_params=_params("parallel"),
        name="in_proj",
    )(x, g, wg, wlr, gw2, gb, wa)


GLA_ROWS_PER_STEP = 4


def _gla_kernel(q_ref, k_ref, lg_ref, v_ref, og_ref, hn_ref, o_ref, s_ref):
    c = pl.program_id(1)

    @pl.when(c == 0)
    def _():
        s_ref[...] = jnp.zeros_like(s_ref)

    for r in range(GLA_ROWS_PER_STEP):
        _gla_chunk(q_ref.at[r], k_ref.at[r], lg_ref.at[r], v_ref.at[r], og_ref.at[r], hn_ref, o_ref.at[r],
                   s_ref.at[r])


def _gla_chunk(q_ref, k_ref, lg_ref, v_ref, og_ref, hn_ref, o_ref, s_ref):
    qk_w, v_w = GLA_HEADS * GLA_DK, GLA_HEADS * GLA_DV
    q, k, g = q_ref[...], k_ref[...], lg_ref[...]
    r_i = lax.broadcasted_iota(I32, (CHUNK, CHUNK), 0)
    c_i = lax.broadcasted_iota(I32, (CHUNK, CHUNK), 1)
    tri = (c_i <= r_i).astype(BF16)
    g_hi = g.astype(BF16)
    g_lo = (g - g_hi.astype(F32)).astype(BF16)
    cum = _dot(tri, g_hi) + _dot(tri, g_lo)
    mid = cum[CHUNK // 2:CHUNK // 2 + 1, :]
    last = cum[CHUNK - 1:CHUNK, :]
    scale = GLA_DK ** -0.5
    qe = q * jnp.exp(cum - mid) * scale
    ke = (k * jnp.exp(mid - cum)).astype(BF16)
    kst = k * jnp.exp(last - cum)
    qc = (q * jnp.exp(cum) * scale).astype(BF16)
    v = v_ref[...].astype(BF16)

    lane_head = lax.broadcasted_iota(I32, (CHUNK, qk_w), 1) // GLA_DK
    q_stack = jnp.concatenate(
        [jnp.where(lane_head == h, qe, 0.0) for h in range(GLA_HEADS)], axis=0).astype(BF16)
    a = lax.dot_general(q_stack, ke, NT_DIMS, preferred_element_type=F32)
    row_in_chunk = lax.broadcasted_iota(I32, (GLA_HEADS * CHUNK, CHUNK), 0) % CHUNK
    col = lax.broadcasted_iota(I32, (GLA_HEADS * CHUNK, CHUNK), 1)
    a = jnp.where(col <= row_in_chunk, a, 0.0).astype(BF16)
    p = _dot(a, v)
    o_intra = jnp.concatenate(
        [p[h * CHUNK:(h + 1) * CHUNK, h * GLA_DV:(h + 1) * GLA_DV] for h in range(GLA_HEADS)], axis=1)

    s = s_ref[...]
    o_inter = _dot(qc, s.astype(BF16))
    kv = _dot(jnp.transpose(kst).astype(BF16), v)
    s_row_head = lax.broadcasted_iota(I32, (qk_w, v_w), 0) // GLA_DK
    s_col_head = lax.broadcasted_iota(I32, (qk_w, v_w), 1) // GLA_DV
    dec = jnp.exp(last)
    dec_col = jnp.transpose(jnp.broadcast_to(dec, (LANES, qk_w)))
    dec_full = jnp.concatenate([dec_col] * (v_w // LANES), axis=1)
    s_ref[...] = s * dec_full + jnp.where(s_row_head == s_col_head, kv, 0.0)

    o = o_intra + o_inter
    og = og_ref[...]
    silu = og * _sigmoid(og)
    outs = []
    for h in range(GLA_HEADS):
        oh = o[:, h * GLA_DV:(h + 1) * GLA_DV]
        outs.append(_rms(oh, hn_ref[...]))
    o_ref[...] = (jnp.concatenate(outs, axis=1) * silu).astype(o_ref.dtype)


def _gla(q, k, lg, v, og, hn):
    b, s, qk_w = q.shape
    v_w = v.shape[-1]
    blk = lambda w: pl.BlockSpec((GLA_ROWS_PER_STEP, CHUNK, w), lambda i, j: (i, j, 0))
    return pl.pallas_call(
        _gla_kernel,
        grid=(b // GLA_ROWS_PER_STEP, s // CHUNK),
        in_specs=[blk(qk_w), blk(qk_w), blk(qk_w), blk(v_w), blk(v_w), _resident(hn.shape)],
        out_specs=blk(v_w),
        out_shape=jax.ShapeDtypeStruct((b, s, v_w), BF16),
        scratch_shapes=[pltpu.VMEM((GLA_ROWS_PER_STEP, qk_w, v_w), F32)],
        compiler_params=_params("parallel", "arbitrary"),
        name="gla",
    )(q, k, lg, v, og, hn)


BAND_CHUNKS_PER_STEP = 4
HEADS_PER_TILE = LANES // ATT_DH


def _band_kernel(q_ref, k_ref, v_ref, bias_ref, o_ref):
    step = pl.program_id(1)
    pad = BAND_CHUNKS * CHUNK
    rows = HEADS_PER_TILE * CHUNK
    low = lax.broadcasted_iota(I32, (CHUNK, LANES), 1) < ATT_DH
    scale = ATT_DH ** -0.5
    for u in range(BAND_CHUNKS_PER_STEP):
        c = step * BAND_CHUNKS_PER_STEP + u
        start = pl.multiple_of(c * CHUNK, CHUNK)
        q = q_ref[u * CHUNK:(u + 1) * CHUNK, :]
        kw = k_ref[pl.ds(start, BAND), :]
        vw = v_ref[pl.ds(start, BAND), :]
        valid = lax.broadcasted_iota(I32, (rows, BAND), 1) >= (pad - c * CHUNK)
        outs = []
        for tile in range(ATT_HEADS // HEADS_PER_TILE):
            sl = slice(tile * LANES, (tile + 1) * LANES)
            q2, k2, v2 = q[:, sl], kw[:, sl], vw[:, sl]
            zero = jnp.zeros_like(q2)
            q_stack = jnp.concatenate([jnp.where(low, q2, zero), jnp.where(low, zero, q2)], axis=0)
            s = lax.dot_general(q_stack, k2, NT_DIMS, preferred_element_type=F32)
            s = jnp.where(valid, s * scale + bias_ref[tile], -1e30)
            e = jnp.exp(s - jnp.max(s, axis=-1, keepdims=True))
            den = jnp.sum(e, axis=-1, keepdims=True)
            pv = _dot(e.astype(BF16), v2) / den
            outs.append(jnp.where(low, pv[0:CHUNK], pv[CHUNK:2 * CHUNK]))
        o_ref[u * CHUNK:(u + 1) * CHUNK, :] = jnp.concatenate(outs, axis=1).astype(o_ref.dtype)


def _band(q, kp, vp, bias):
    b, s, w = q.shape
    sp = kp.shape[1]
    rows = BAND_CHUNKS_PER_STEP * CHUNK
    return pl.pallas_call(
        _band_kernel,
        grid=(b, s // rows),
        in_specs=[pl.BlockSpec((None, rows, w), lambda i, j: (i, j, 0)),
                  pl.BlockSpec((None, sp, w), lambda i, j: (i, 0, 0)),
                  pl.BlockSpec((None, sp, w), lambda i, j: (i, 0, 0)),
                  _resident(bias.shape)],
        out_specs=pl.BlockSpec((None, rows, w), lambda i, j: (i, j, 0)),
        out_shape=jax.ShapeDtypeStruct((b, s, w), BF16),
        compiler_params=_params("parallel", "arbitrary"),
        name="band",
    )(q, kp, vp, bias)


def _mix_out_kernel(og_ref, oa_ref, h_ref, wo_ref, gf_ref, wq_ref, h2_ref, xn_ref, qp_ref):
    v_w = og_ref.shape[-1]
    o = _dot(og_ref[...], wo_ref[0:v_w, :]) + _dot(oa_ref[...], wo_ref[v_w:, :])
    h2 = h_ref[...] + o
    xn = _rms(h2, gf_ref[...])
    for c in range(h2_ref.shape[1]):
        h2_ref[:, c, :] = h2[:, c * LANES:(c + 1) * LANES]
        xn_ref[:, c, :] = xn[:, c * LANES:(c + 1) * LANES]
    qp = _dot(xn.astype(BF16), wq_ref[...])
    for i in range(qp_ref.shape[0]):
        qp_ref[i] = qp[:, i * LANES:(i + 1) * LANES].astype(BF16)


def _tiled_rows(tm, d):
    return pl.BlockSpec((tm, d // LANES, LANES), lambda i: (i, 0, 0))


def _mix_out(og, oa, h, wo, gf, wq, tm):
    t, d = h.shape
    n_sub = wq.shape[1] // LANES
    row = lambda w: pl.BlockSpec((tm, w), lambda i: (i, 0))
    return pl.pallas_call(
        _mix_out_kernel,
        grid=(t // tm,),
        in_specs=[row(og.shape[1]), row(oa.shape[1]), row(d), _resident(wo.shape), _resident(gf.shape),
                  _resident(wq.shape)],
        out_specs=[_tiled_rows(tm, d), _tiled_rows(tm, d), pl.BlockSpec((n_sub, tm, LANES), lambda i: (0, i, 0))],
        out_shape=[jax.ShapeDtypeStruct((t, d // LANES, LANES), F32),
                   jax.ShapeDtypeStruct((t, d // LANES, LANES), F32),
                   jax.ShapeDtypeStruct((n_sub, t, LANES), BF16)],
        compiler_params=_params("parallel"),
        name="mix_out",
    )(og, oa, h, wo, gf, wq)


_GRID_REGS = (
    ("a", 0, 0, 0, 8), ("a", 0, 8, 0, 8), ("a", 1, 0, 0, 8), ("b", 0, 8, 0, 8),
    ("a", 2, 0, 0, 5), ("a", 3, 0, 0, 4), ("b", 0, 0, 4, 8), ("b", 1, 0, 4, 8), ("b", 2, 0, 4, 5),
)


TOPK_HEAD_UNROLL = 8


def _top_rows(s, iota_rows, count):
    n = s.shape[0]
    vals, ids = [], []
    for _ in range(count):
        m = jnp.max(s, axis=0, keepdims=True)
        idx = jnp.min(jnp.where(s == m, iota_rows, float(n)), axis=0, keepdims=True)
        s = jnp.where(iota_rows == idx, -jnp.inf, s)
        vals.append(m)
        ids.append(idx)
    return vals, ids


def _peer_topk_kernel(q_ref, sk_ref, off_ref, sh_ref, gate_ref, s_scr, i_scr, t_scr, e_scr, g_all, o_all, h_all):
    ntok = q_ref.shape[1]
    iota_keys = lax.broadcasted_iota(I32, (N_KEYS, ntok), 0).astype(F32)
    sub = lax.broadcasted_iota(I32, (SUBLANES, ntok), 0)
    subf = sub.astype(F32)

    def one_head(h, u):
        for half in range(2):
            sc = lax.dot_general(sk_ref[2 * h + half], q_ref[2 * h + half], NT_DIMS,
                                 preferred_element_type=F32)
            vals, ids = _top_rows(sc, iota_keys, PEER_TOPK)
            for kk in range(PEER_TOPK):
                s_scr[2 * u + half, kk:kk + 1, :] = vals[kk]
                i_scr[2 * u + half, kk:kk + 1, :] = ids[kk]
        cand, flat, eid = [], [], []
        for axis, fixed, first, lo, hi in _GRID_REGS:
            run = slice(first, first + SUBLANES)
            if axis == "a":
                val = s_scr[2 * u, fixed:fixed + 1, :] + s_scr[2 * u + 1, run, :]
                ee = i_scr[2 * u, fixed:fixed + 1, :] * float(N_KEYS) + i_scr[2 * u + 1, run, :]
                ff = float(fixed * PEER_TOPK + first) + subf
            else:
                val = s_scr[2 * u, run, :] + s_scr[2 * u + 1, fixed:fixed + 1, :]
                ee = i_scr[2 * u, run, :] * float(N_KEYS) + i_scr[2 * u + 1, fixed:fixed + 1, :]
                ff = (float(first) + subf) * float(PEER_TOPK) + float(fixed)
            if (lo, hi) != (0, SUBLANES):
                val = jnp.where((sub >= lo) & (sub < hi), val, -jnp.inf)
            cand.append(val)
            flat.append(ff)
            eid.append(ee)
        big = float(PEER_TOPK * PEER_TOPK)
        for kk in range(PEER_TOPK):
            m = jnp.max(functools.reduce(jnp.maximum, cand), axis=0, keepdims=True)
            fsel = [jnp.where(cv == m, fv, big) for cv, fv in zip(cand, flat)]
            fm = jnp.min(functools.reduce(jnp.minimum, fsel), axis=0, keepdims=True)
            hit = [fv == fm for fv in flat]
            esel = [jnp.where(hv, ev, -1.0) for hv, ev in zip(hit, eid)]
            e = jnp.max(functools.reduce(jnp.maximum, esel), axis=0, keepdims=True)
            cand = [jnp.where(hv, -jnp.inf, cv) for hv, cv in zip(hit, cand)]
            t_scr[u, kk:kk + 1, :] = m
            e_scr[u, kk:kk + 1, :] = e
        top = t_scr[u]
        ex = jnp.exp(top - top[0:1, :])
        gate = ex / jnp.sum(ex, axis=0, keepdims=True)
        e = e_scr[u]
        in_low_half = (e >= float(HALF_EXPERTS)).astype(F32)
        rows = pl.ds(pl.multiple_of(h * PEER_TOPK, PEER_TOPK), PEER_TOPK)
        g_all[rows, :] = gate
        o_all[rows, :] = (e - in_low_half * float(HALF_EXPERTS)) * float(SUBLANES)
        h_all[rows, :] = in_low_half * 16.0

    def head_group(g, carry):
        for u in range(TOPK_HEAD_UNROLL):
            one_head(g * TOPK_HEAD_UNROLL + u, u)
        return carry

    lax.fori_loop(0, PEER_HEADS // TOPK_HEAD_UNROLL, head_group, 0)
    gate_ref[...] = jnp.transpose(g_all[...])
    off_ref[...] = o_all[...].astype(I32)
    sh_ref[...] = jnp.transpose(h_all[...]).astype(I32)


def _peer_topk(qp, sk):
    n_sub, t, _ = qp.shape
    ntok = PEER_BLOCK
    out = pl.BlockSpec((ntok, PEER_SLOTS), lambda i: (i, 0))
    return pl.pallas_call(
        _peer_topk_kernel,
        grid=(t // ntok,),
        in_specs=[pl.BlockSpec((n_sub, ntok, LANES), lambda i: (0, i, 0)), _resident(sk.shape)],
        out_specs=[pl.BlockSpec((None, PEER_SLOTS, ntok), lambda i: (i, 0, 0)), out, out],
        out_shape=[jax.ShapeDtypeStruct((t // ntok, PEER_SLOTS, ntok), I32),
                   jax.ShapeDtypeStruct((t, PEER_SLOTS), I32), jax.ShapeDtypeStruct((t, PEER_SLOTS), F32)],
        scratch_shapes=[pltpu.VMEM((2 * TOPK_HEAD_UNROLL, PEER_TOPK, ntok), F32),
                        pltpu.VMEM((2 * TOPK_HEAD_UNROLL, PEER_TOPK, ntok), F32),
                        pltpu.VMEM((TOPK_HEAD_UNROLL, PEER_TOPK, ntok), F32),
                        pltpu.VMEM((TOPK_HEAD_UNROLL, PEER_TOPK, ntok), F32),
                        pltpu.VMEM((PEER_SLOTS, ntok), F32), pltpu.VMEM((PEER_SLOTS, ntok), F32),
                        pltpu.VMEM((PEER_SLOTS, ntok), F32)],
        compiler_params=_params("parallel"),
        name="peer_topk",
    )(qp, sk)


TOKEN_GROUP = SUBLANES
BLOCKS_PER_STEP = 2


def _offsets_copy(off_hbm, blk, sm_ref, sem):
    return pltpu.make_async_copy(off_hbm.at[blk], sm_ref, sem)


def _for_each_offset_block(off_hbm, sm_refs, sems, body):
    step, nsteps = pl.program_id(0), pl.num_programs(0)
    first = step * BLOCKS_PER_STEP

    @pl.when(step == 0)
    def _():
        _offsets_copy(off_hbm, 0, sm_refs[0], sems.at[0]).start()

    for b in range(BLOCKS_PER_STEP):
        _offsets_copy(off_hbm, first + b, sm_refs[b], sems.at[b]).wait()
        if b + 1 < BLOCKS_PER_STEP:
            _offsets_copy(off_hbm, first + b + 1, sm_refs[b + 1], sems.at[b + 1]).start()
        else:
            @pl.when(step + 1 < nsteps)
            def _():
                _offsets_copy(off_hbm, first + BLOCKS_PER_STEP, sm_refs[0], sems.at[0]).start()
        body(b, sm_refs[b])


def _expert_rows(tab_ref, sm_ref, shm_ref, tok, j, slot):
    o = pl.multiple_of(sm_ref.at[slot][tok], SUBLANES)
    w = tab_ref[pl.ds(o, SUBLANES), :]
    w = jnp.left_shift(w, shm_ref[j, slot:slot + 1, :]) & HI_MASK
    return lax.bitcast_convert_type(w, F32)


def _lane_broadcast_rows(rows, j):
    return jnp.transpose(jnp.broadcast_to(rows[j:j + 1, :], (LANES, rows.shape[1])))


def _peer_u_kernel(off_hbm, x_ref, sh_ref, pick_ref, fold_ref, tab_ref, act_ref,
                   sm_a, sm_b, sems, shm_a, shm_b, prod_a, prod_b):
    pairs = PEER_BLOCK // (2 * TOKEN_GROUP)

    @pl.when(pl.program_id(0) == 0)
    def _():
        prod_b[...] = jnp.zeros_like(prod_b)

    def block(b, sm_ref):
        base = b * PEER_BLOCK

        def fill(g, shm_ref, prod_ref):
            t0 = pl.multiple_of(g * TOKEN_GROUP, TOKEN_GROUP)
            sh_rows = sh_ref[pl.ds(base + t0, TOKEN_GROUP), :]
            for j in range(TOKEN_GROUP):
                tok = t0 + j
                x = x_ref[pl.ds(pl.multiple_of((base + tok) * SUBLANES, SUBLANES), SUBLANES), :]
                shm_ref[j] = _lane_broadcast_rows(sh_rows, j)
                for slot in range(0, PEER_SLOTS, 2):
                    two = jnp.concatenate([_expert_rows(tab_ref, sm_ref, shm_ref, tok, j, slot) * x,
                                           _expert_rows(tab_ref, sm_ref, shm_ref, tok, j, slot + 1) * x], axis=0)
                    prod_ref[slot * SUBLANES:(slot + 2) * SUBLANES, j * LANES:(j + 1) * LANES] = two.astype(BF16)

        def reduce(g, prod_ref):
            lane_sums = lax.dot_general(pick_ref[...], prod_ref[...], NT_DIMS,
                                        preferred_element_type=F32)
            hi = lane_sums.astype(BF16).astype(F32)
            both = jnp.concatenate([hi, lane_sums - hi], axis=0).astype(BF16)
            res = _dot(both, fold_ref[...])
            rows = pl.ds(pl.multiple_of(base + g * TOKEN_GROUP, TOKEN_GROUP), TOKEN_GROUP)
            act_ref[rows, :] = res[0:TOKEN_GROUP] + res[TOKEN_GROUP:2 * TOKEN_GROUP]

        def pair(i, carry):
            reduce(jnp.maximum(2 * i - 1, 0), prod_b)
            fill(2 * i, shm_a, prod_a)
            reduce(2 * i, prod_a)
            fill(2 * i + 1, shm_b, prod_b)
            return carry

        lax.fori_loop(0, pairs, pair, 0)
        reduce(2 * pairs - 1, prod_b)

    _for_each_offset_block(off_hbm, (sm_a, sm_b), sems, block)


def _gather_scratch():
    sm = pltpu.SMEM((PEER_SLOTS, PEER_BLOCK), I32)
    return [sm, sm, pltpu.SemaphoreType.DMA((BLOCKS_PER_STEP,))]


def _peer_u(off, x8, sh, tab):
    t = sh.shape[0]
    ntok = BLOCKS_PER_STEP * PEER_BLOCK
    width, rows = TOKEN_GROUP * LANES, PEER_SLOTS * SUBLANES
    pick = (jnp.arange(width)[None, :] // LANES == jnp.arange(TOKEN_GROUP)[:, None]).astype(BF16)
    fold = (jnp.arange(rows)[:, None] // SUBLANES == jnp.arange(PEER_SLOTS)[None, :]).astype(BF16)
    slots = pl.BlockSpec((ntok, PEER_SLOTS), lambda i: (i, 0))
    shm = pltpu.VMEM((TOKEN_GROUP, PEER_SLOTS, LANES), I32)
    prod = pltpu.VMEM((rows, width), BF16)
    return pl.pallas_call(
        _peer_u_kernel,
        grid=(t // ntok,),
        in_specs=[pl.BlockSpec(memory_space=pl.ANY),
                  pl.BlockSpec((ntok * SUBLANES, LANES), lambda i: (i, 0)),
                  slots, _resident(pick.shape), _resident(fold.shape), _resident(tab.shape)],
        out_specs=slots,
        out_shape=jax.ShapeDtypeStruct((t, PEER_SLOTS), F32),
        scratch_shapes=_gather_scratch() + [shm, shm, prod, prod],
        compiler_params=_params("arbitrary"),
        name="peer_u",
    )(off, x8, sh, pick, fold, tab)


def _peer_v_kernel(off_hbm, act_ref, gate_ref, sh_ref, h_ref, tab_ref, out_ref,
                   sm_a, sm_b, sems, shm_a, shm_b, hm_a, hm_b, hid_ref):
    act = act_ref[...]
    gelu = 0.5 * act * (1.0 + lax.erf(act * (2.0 ** -0.5)))
    hid_ref[...] = gate_ref[...] * gelu
    n_acc = 4
    groups = PEER_BLOCK // TOKEN_GROUP

    def block(b, sm_ref):
        base = b * PEER_BLOCK

        def spread(g, shm_ref, hm_ref):
            t0 = pl.multiple_of(g * TOKEN_GROUP, TOKEN_GROUP)
            sh_rows = sh_ref[pl.ds(base + t0, TOKEN_GROUP), :]
            hid_rows = hid_ref[pl.ds(base + t0, TOKEN_GROUP), :]
            for j in range(TOKEN_GROUP):
                shm_ref[j] = _lane_broadcast_rows(sh_rows, j)
                hm_ref[j] = _lane_broadcast_rows(hid_rows, j)

        def gather(g, shm_ref, hm_ref):
            t0 = pl.multiple_of(g * TOKEN_GROUP, TOKEN_GROUP)
            for j in range(TOKEN_GROUP):
                tok = t0 + j
                accs = [None] * n_acc
                for slot in range(PEER_SLOTS):
                    term = _expert_rows(tab_ref, sm_ref, shm_ref, tok, j, slot) * hm_ref[j, slot:slot + 1, :]
                    accs[slot % n_acc] = term if accs[slot % n_acc] is None else accs[slot % n_acc] + term
                rows = pl.ds(pl.multiple_of((base + tok) * SUBLANES, SUBLANES), SUBLANES)
                out_ref[rows, :] = h_ref[rows, :] + ((accs[0] + accs[1]) + (accs[2] + accs[3]))

        def pair(i, carry):
            spread(2 * i + 1, shm_b, hm_b)
            gather(2 * i, shm_a, hm_a)
            spread(jnp.minimum(2 * i + 2, groups - 1), shm_a, hm_a)
            gather(2 * i + 1, shm_b, hm_b)
            return carry

        spread(0, shm_a, hm_a)
        lax.fori_loop(0, groups // 2, pair, 0)

    _for_each_offset_block(off_hbm, (sm_a, sm_b), sems, block)


def _peer_v(off, act, gate, sh, h8, tab):
    t = sh.shape[0]
    ntok = BLOCKS_PER_STEP * PEER_BLOCK
    slots = pl.BlockSpec((ntok, PEER_SLOTS), lambda i: (i, 0))
    rows8 = pl.BlockSpec((ntok * SUBLANES, LANES), lambda i: (i, 0))
    return pl.pallas_call(
        _peer_v_kernel,
        grid=(t // ntok,),
        in_specs=[pl.BlockSpec(memory_space=pl.ANY), slots, slots, slots, rows8, _resident(tab.shape)],
        out_specs=rows8,
        out_shape=jax.ShapeDtypeStruct(h8.shape, F32),
        scratch_shapes=_gather_scratch() + [pltpu.VMEM((TOKEN_GROUP, PEER_SLOTS, LANES), I32),
                                            pltpu.VMEM((TOKEN_GROUP, PEER_SLOTS, LANES), I32),
                                            pltpu.VMEM((TOKEN_GROUP, PEER_SLOTS, LANES), F32),
                                            pltpu.VMEM((TOKEN_GROUP, PEER_SLOTS, LANES), F32),
                                            pltpu.VMEM((ntok, PEER_SLOTS), F32)],
        compiler_params=_params("arbitrary"),
        name="peer_v",
    )(off, act, gate, sh, h8, tab)


def _ple_kernel(h_ref, p_ref, g_ref, wg_ref, wp_ref, fin_ref, out_ref, *, final):
    h = jnp.concatenate([h_ref[:, c, :] for c in range(h_ref.shape[1])], axis=1)
    gate = _sigmoid(_dot(_rms(h, g_ref[...]).astype(BF16), wg_ref[...]))
    h2 = h + _dot(p_ref[...].astype(BF16), wp_ref[...]) * gate
    if final:
        h2 = _rms(h2, fin_ref[...])
    out_ref[...] = h2


def _ple(h3, p, layer, g, wg, wp, fin, tm, final):
    t, d = h3.shape[0], h3.shape[1] * h3.shape[2]
    row = lambda w: pl.BlockSpec((tm, w), lambda i: (i, 0))
    return pl.pallas_call(
        functools.partial(_ple_kernel, final=final),
        grid=(t // tm,),
        in_specs=[_tiled_rows(tm, d), pl.BlockSpec((None, tm, p.shape[2]), lambda i: (layer, i, 0)),
                  _resident(g.shape), _resident(wg.shape), _resident(wp.shape),
                  _resident(fin.shape)],
        out_specs=row(d),
        out_shape=jax.ShapeDtypeStruct((t, d), F32),
        compiler_params=_params("parallel"),
        name="ple",
    )(h3, p, g, wg, wp, fin)


def _pack_table(tab):
    e, d = tab.shape
    bits = lax.bitcast_convert_type(tab.astype(BF16), jnp.uint16).astype(jnp.uint32)
    word = (bits[: e // 2] << 16) | bits[e // 2:]
    return lax.bitcast_convert_type(word, I32).reshape(e // 2 * (d // LANES), LANES)


def _rel_bias(rel_table):
    dist = jnp.arange(CHUNK - 1 + BAND_CHUNKS * CHUNK, -CHUNK, -1)
    line = rel_table.astype(F32)[:, jnp.clip(dist, -REL_CLIP, REL_CLIP) + REL_CLIP]
    bias = jnp.stack([line[:, CHUNK - 1 - i:CHUNK - 1 - i + BAND] for i in range(CHUNK)], axis=1)
    return bias.reshape(ATT_HEADS // HEADS_PER_TILE, HEADS_PER_TILE * CHUNK, BAND)


def kernel(x, p, norm_mix, w_in, gla_gate_w2, gla_gate_b, gla_head_norm, rel_bias, w_out, norm_ffn, peer_wq,
           peer_subkeys, peer_u, peer_v, norm_ple, w_ple_gate, w_ple, final_norm):
    b, s, d = x.shape
    depth = w_in.shape[0]
    t = b * s
    qk_w, v_w, a_w = GLA_HEADS * GLA_DK, GLA_HEADS * GLA_DV, ATT_HEADS * ATT_DH
    tm = 512
    h = x.reshape(t, d)
    fin = final_norm.reshape(1, d)
    for i in range(depth):
        w = w_in[i]
        c0 = 2 * qk_w + v_w
        wg = jnp.concatenate([w[:, :c0], w[:, c0 + GLA_RANK:c0 + GLA_RANK + v_w]], axis=1).astype(BF16)
        wlr = jnp.pad(w[:, c0:c0 + GLA_RANK], ((0, 0), (0, LANES - GLA_RANK))).astype(BF16)
        wa = w[:, c0 + GLA_RANK + v_w:].astype(BF16)
        gw2 = jnp.pad(gla_gate_w2[i], ((0, LANES - GLA_RANK), (0, 0))).astype(BF16)
        q, k, lg, v, og, aq, ak, av = _in_proj(h, norm_mix[i].reshape(1, d), wg, wlr, gw2,
                                               gla_gate_b[i].reshape(1, qk_w), wa, tm)
        to3 = lambda z: z.reshape(b, s, z.shape[-1])
        o_gla = _gla(to3(q), to3(k), to3(lg), to3(v), to3(og), gla_head_norm[i].reshape(1, GLA_DV))
        front = ((0, 0), (BAND_CHUNKS * CHUNK, 0), (0, 0))
        o_att = _band(to3(aq), jnp.pad(to3(ak), front), jnp.pad(to3(av), front), _rel_bias(rel_bias[i]))
        h, xn, qp = _mix_out(o_gla.reshape(t, v_w), o_att.reshape(t, a_w), h, w_out[i].astype(BF16),
                             norm_ffn[i].reshape(1, d), peer_wq[i].astype(BF16), tm)
        sk = peer_subkeys[i].reshape(PEER_HEADS * 2, N_KEYS, -1).astype(BF16)
        off, sh, gate = _peer_topk(qp, sk)
        rows8 = (t * (d // LANES), LANES)
        act = _peer_u(off, xn.reshape(rows8), sh, _pack_table(peer_u[i]))
        h = _peer_v(off, act, gate, sh, h.reshape(rows8), _pack_table(peer_v[i]))
        h = _ple(h.reshape(t, d // LANES, LANES), p.reshape(depth, t, -1), i, norm_ple[i].reshape(1, d),
                 w_ple_gate[i].astype(BF16), w_ple[i].astype(BF16), fin, tm, final=(i == depth - 1))
    return h.reshape(b, s, d)
```
